```python
import jax, jax.numpy as jnp
from jax import lax
import numpy as np

D_MODEL = 1024
BATCH = 4
SEQ = 8192
DEPTH = 4

N_MIXERS = 3
EPS = 1e-6

RET_HEADS = 4
RET_QK_DIM = D_MODEL // RET_HEADS
RET_V_DIM = 2 * RET_QK_DIM
RET_CHUNK = 128
RET_ROPE_BASE = 10000.0
RET_IN = 2 * RET_HEADS * RET_QK_DIM + 2 * RET_HEADS * RET_V_DIM

GM_CHUNK = 128
GM_FFN = 6 * D_MODEL
GM_HALF = GM_FFN // 2
GM_GROUPS = 4
GM_GROUP_DIM = GM_HALF // GM_GROUPS

GRID_W = 64
NA_HEADS = 32
NA_HEAD_DIM = D_MODEL // NA_HEADS
NA_WIN_ROWS = 8
NA_WIN_COLS = 16

N_GROUPS = 4
EXPERTS_PER_GROUP = 4
N_EXPERTS = N_GROUPS * EXPERTS_PER_GROUP
TOP_K = 2
D_EXPERT = 512

kernel_name = 'hybrid_retention_gmlp_natten_grouped_moe_encoder'


def _rmsnorm(x, g):
    xf = x.astype(jnp.float32)
    y = xf * lax.rsqrt(jnp.mean(xf * xf, axis=-1, keepdims=True) + EPS)
    return (y * g.astype(jnp.float32)).astype(x.dtype)


def _layernorm(x, g, b):
    xf = x.astype(jnp.float32)
    mu = jnp.mean(xf, axis=-1, keepdims=True)
    var = jnp.mean(jnp.square(xf - mu), axis=-1, keepdims=True)
    y = (xf - mu) * lax.rsqrt(var + EPS)
    return (y * g.astype(jnp.float32) + b.astype(jnp.float32)).astype(x.dtype)


def _rotary(x, pos):
    half = x.shape[-1] // 2
    inv_freq = RET_ROPE_BASE ** (-jnp.arange(half, dtype=jnp.float32) / half)
    ang = pos[:, None] * inv_freq[None, :]
    cos = jnp.cos(ang)[None, :, None, :].astype(x.dtype)
    sin = jnp.sin(ang)[None, :, None, :].astype(x.dtype)
    x1, x2 = x[..., :half], x[..., half:]
    return jnp.concatenate([x1 * cos - x2 * sin, x1 * sin + x2 * cos], axis=-1)


def _retention_causal(q, k, v, log_gamma, include_diag):
    bsz, nh, seq, dk = q.shape
    dv = v.shape[-1]
    n_chunks = seq // RET_CHUNK
    qc = q.reshape(bsz, nh, n_chunks, RET_CHUNK, dk)
    kc = k.reshape(bsz, nh, n_chunks, RET_CHUNK, dk)
    vc = v.reshape(bsz, nh, n_chunks, RET_CHUNK, dv)
    idx = jnp.arange(RET_CHUNK, dtype=jnp.float32)
    diff = idx[:, None] - idx[None, :]
    mask = (diff >= 0) if include_diag else (diff > 0)
    decay = jnp.where(mask[None], jnp.exp(log_gamma[:, None, None] * jnp.where(mask, diff, 0.0)[None]), 0.0)
    scores = jnp.einsum('bhncd,bhnmd->bhncm', qc, kc) * decay[None, :, None]
    inner = jnp.einsum('bhncm,bhnme->bhnce', scores, vc)
    xi = jnp.exp(log_gamma[:, None] * (idx[None, :] + 1.0))
    zeta = jnp.exp(log_gamma[:, None] * (RET_CHUNK - 1.0 - idx[None, :]))
    chunk_decay = jnp.exp(log_gamma * RET_CHUNK)

    def step(state, inp):
        q_i, k_i, v_i = inp
        cross = jnp.einsum('bhcd,bhde->bhce', q_i, state) * xi[None, :, :, None]
        state = state * chunk_decay[None, :, None, None] + jnp.einsum(
            'bhcd,bhce->bhde', k_i, v_i * zeta[None, :, :, None])
        return state, cross

    state0 = jnp.zeros((bsz, nh, dk, dv), jnp.float32)
    _, cross = lax.scan(step, state0, (jnp.moveaxis(qc, 2, 0), jnp.moveaxis(kc, 2, 0), jnp.moveaxis(vc, 2, 0)))
    out = inner + jnp.moveaxis(cross, 0, 2)
    return out.reshape(bsz, nh, seq, dv)


def _retention_mixer(h, w_in, decay_logit, w_out):
    bsz, seq, _ = h.shape
    nq = RET_HEADS * RET_QK_DIM
    nv = RET_HEADS * RET_V_DIM
    proj = h @ w_in
    q = proj[..., :nq].reshape(bsz, seq, RET_HEADS, RET_QK_DIM)
    k = proj[..., nq:2 * nq].reshape(bsz, seq, RET_HEADS, RET_QK_DIM)
    v = proj[..., 2 * nq:2 * nq + nv].reshape(bsz, seq, RET_HEADS, RET_V_DIM)
    gate = proj[..., 2 * nq + nv:]
    pos = jnp.arange(seq, dtype=jnp.float32)
    q = _rotary(q, pos)
    k = _rotary(k, pos) * (RET_QK_DIM ** -0.5)
    q, k, v = (jnp.swapaxes(t, 1, 2) for t in (q, k, v))
    log_gamma = jax.nn.log_sigmoid(decay_logit.astype(jnp.float32))
    fwd = _retention_causal(q, k, v, log_gamma[0], True)
    bwd = jnp.flip(_retention_causal(jnp.flip(q, 2), jnp.flip(k, 2), jnp.flip(v, 2), log_gamma[1], False), 2)
    o = (fwd + bwd).astype(jnp.float32)
    o = o * lax.rsqrt(jnp.mean(o * o, axis=-1, keepdims=True) + EPS)
    o = jnp.swapaxes(o, 1, 2).reshape(bsz, seq, nv).astype(h.dtype)
    return (jax.nn.silu(gate) * o) @ w_out


def _gmlp_mixer(h, w_in, ln_g, ln_b, w_s, b_s, w_out):
    bsz, seq, _ = h.shape
    n_chunks = seq // GM_CHUNK
    z = jax.nn.gelu(h @ w_in, approximate=False)
    u, v = z[..., :GM_HALF], z[..., GM_HALF:]
    v = _layernorm(v, ln_g, ln_b).reshape(bsz, n_chunks, GM_CHUNK, GM_GROUPS, GM_GROUP_DIM)
    mixed = jnp.einsum('gts,bnsgc->bntgc', w_s, v) + jnp.swapaxes(b_s, 0, 1)[None, None, :, :, None]
    return (u * mixed.reshape(bsz, seq, GM_HALF)) @ w_out


def _neighbourhood_mixer(h, w_qkv, rpb, w_out):
    bsz, seq, _ = h.shape
    rows = seq // GRID_W
    kr = min(NA_WIN_ROWS, rows)
    kc = NA_WIN_COLS
    qkv = (h @ w_qkv).reshape(bsz, rows, GRID_W, 3, NA_HEADS, NA_HEAD_DIM)
    q = qkv[:, :, :, 0] * (NA_HEAD_DIM ** -0.5)
    k = qkv[:, :, :, 1]
    v = qkv[:, :, :, 2]
    col = jnp.arange(GRID_W)
    col_idx = jnp.clip(col - kc // 2, 0, GRID_W - kc)[:, None] + jnp.arange(kc)[None, :]
    rpb_cols = rpb[:, :, col_idx - col[:, None] + kc - 1]

    def row_block(args):
        r, q_row = args
        r0 = jnp.clip(r - kr // 2, 0, rows - kr)
        k_win = lax.dynamic_slice_in_dim(k, r0, kr, axis=1)[:, :, col_idx]
        v_win = lax.dynamic_slice_in_dim(v, r0, kr, axis=1)[:, :, col_idx]
        row_off = r0 + jnp.arange(kr) - r + NA_WIN_ROWS - 1
        bias = jnp.transpose(rpb_cols[:, row_off], (0, 2, 1, 3)).astype(jnp.float32)
        s = jnp.einsum('bqhd,brqchd->bhqrc', q_row, k_win).astype(jnp.float32) + bias[None]
        p = jax.nn.softmax(s.reshape(bsz, NA_HEADS, GRID_W, kr * kc), axis=-1)
        p = p.reshape(bsz, NA_HEADS, GRID_W, kr, kc).astype(v.dtype)
        return jnp.einsum('bhqrc,brqchd->bqhd', p, v_win)

    out = lax.map(row_block, (jnp.arange(rows, dtype=jnp.int32), jnp.moveaxis(q, 1, 0)))
    out = jnp.moveaxis(out, 0, 1).reshape(bsz, seq, D_MODEL)
    return out @ w_out


def _grouped_moe(h, router_w, router_b, w_gate_up, w_down):
    bsz, seq, d = h.shape
    t = h.reshape(bsz * seq, d)
    scores = jax.nn.sigmoid((t @ router_w).astype(jnp.float32))
    biased = scores + router_b.astype(jnp.float32)
    grp_top = lax.top_k(biased.reshape(-1, N_GROUPS, EXPERTS_PER_GROUP), TOP_K)[0]
    grp_sel = jnp.argmax(jnp.sum(grp_top, axis=-1), axis=-1)
    in_grp = (jnp.arange(N_EXPERTS) // EXPERTS_PER_GROUP)[None, :] == grp_sel[:, None]
    _, top_idx = lax.top_k(jnp.where(in_grp, biased, -jnp.inf), TOP_K)
    sel = jnp.take_along_axis(scores, top_idx, axis=-1)
    w = sel / jnp.sum(sel, axis=-1, keepdims=True)
    gates = jnp.sum(jax.nn.one_hot(top_idx, N_EXPERTS, dtype=jnp.float32) * w[..., None], axis=1).astype(t.dtype)
    y = jnp.zeros_like(t)
    for e in range(N_EXPERTS):
        gu = t @ w_gate_up[e]
        hid = jax.nn.silu(gu[:, :D_EXPERT]) * gu[:, D_EXPERT:]
        y = y + gates[:, e:e + 1] * (hid @ w_down[e])
    return y.reshape(bsz, seq, d)


def setup_inputs(seed: int = 0) -> dict:
    key = jax.random.key(seed)
    ks = jax.random.split(key, 24)
    f32 = jnp.float32
    n_ret = len(range(0, DEPTH, N_MIXERS))
    n_gm = len(range(1, DEPTH, N_MIXERS))
    n_na = len(range(2, DEPTH, N_MIXERS))

    def nrm(k, shape, scale):
        return jax.random.normal(k, shape, f32) * scale

    base_logit = np.log(2.0 ** (5.0 + np.arange(RET_HEADS)) - 1.0).astype(np.float32)
    return {
        'x': nrm(ks[0], (BATCH, SEQ, D_MODEL), 1.0),
        'c': nrm(ks[1], (BATCH, D_MODEL), 1.0),
        'norm_g': 1.0 + nrm(ks[2], (DEPTH, 2, D_MODEL), 0.02),
        'final_g': 1.0 + nrm(ks[3], (D_MODEL,), 0.02),
        'w_ada': nrm(ks[4], (DEPTH, D_MODEL, 6 * D_MODEL), 0.5 * D_MODEL ** -0.5),
        'b_ada': nrm(ks[5], (DEPTH, 6 * D_MODEL), 0.02),
        'ret_w_in': nrm(ks[6], (n_ret, D_MODEL, RET_IN), D_MODEL ** -0.5),
        'ret_decay_logit': jnp.asarray(base_logit)[None, None, :] + nrm(ks[7], (n_ret, 2, RET_HEADS), 0.1),
        'ret_w_out': nrm(ks[8], (n_ret, RET_HEADS * RET_V_DIM, D_MODEL), (RET_HEADS * RET_V_DIM) ** -0.5),
        'gm_w_in': nrm(ks[9], (n_gm, D_MODEL, GM_FFN), D_MODEL ** -0.5),
        'gm_ln_g': 1.0 + nrm(ks[10], (n_gm, GM_HALF), 0.02),
        'gm_ln_b': nrm(ks[11], (n_gm, GM_HALF), 0.02),
        'gm_w_s': nrm(ks[12], (n_gm, GM_GROUPS, GM_CHUNK, GM_CHUNK), GM_CHUNK ** -0.5),
        'gm_b_s': 1.0 + nrm(ks[13], (n_gm, GM_GROUPS, GM_CHUNK), 0.1),
        'gm_w_out': nrm(ks[14], (n_gm, GM_HALF, D_MODEL), GM_HALF ** -0.5),
        'na_w_qkv': nrm(ks[15], (n_na, D_MODEL, 3 * D_MODEL), D_MODEL ** -0.5),
        'na_rpb': nrm(ks[16], (n_na, NA_HEADS, 2 * NA_WIN_ROWS - 1, 2 * NA_WIN_COLS - 1), 0.05),
        'na_w_out': nrm(ks[17], (n_na, D_MODEL, D_MODEL), D_MODEL ** -0.5),
        'router_w': nrm(ks[18], (D_MODEL, N_EXPERTS), D_MODEL ** -0.5),
        'router_b': nrm(ks[19], (N_EXPERTS,), 0.01),
        'moe_w_gate_up': nrm(ks[20], (DEPTH, N_EXPERTS, D_MODEL, 2 * D_EXPERT), D_MODEL ** -0.5),
        'moe_w_down': nrm(ks[21], (DEPTH, N_EXPERTS, D_EXPERT, D_MODEL), D_EXPERT ** -0.5),
    }


def reference(x, c, norm_g, final_g, w_ada, b_ada, ret_w_in, ret_decay_logit, ret_w_out,
              gm_w_in, gm_ln_g, gm_ln_b, gm_w_s, gm_b_s, gm_w_out, na_w_qkv, na_rpb, na_w_out,
              router_w, router_b, moe_w_gate_up, moe_w_down):
    c_act = jax.nn.silu(c)
    for i in range(DEPTH):
        kind, j = i % N_MIXERS, i // N_MIXERS
        mod = c_act @ w_ada[i] + b_ada[i]
        sh1, sc1, g1, sh2, sc2, g2 = [m[:, None, :] for m in jnp.split(mod, 6, axis=-1)]
        h = _rmsnorm(x, norm_g[i, 0]) * (1 + sc1) + sh1
        if kind == 0:
            y = _retention_mixer(h, ret_w_in[j], ret_decay_logit[j], ret_w_out[j])
        elif kind == 1:
            y = _gmlp_mixer(h, gm_w_in[j], gm_ln_g[j], gm_ln_b[j], gm_w_s[j], gm_b_s[j], gm_w_out[j])
        else:
            y = _neighbourhood_mixer(h, na_w_qkv[j], na_rpb[j], na_w_out[j])
        x = x + g1 * y
        h = _rmsnorm(x, norm_g[i, 1]) * (1 + sc2) + sh2
        x = x + g2 * _grouped_moe(h, router_w, router_b, moe_w_gate_up[i], moe_w_down[i])
    return _rmsnorm(x, final_g)
```

```python
import functools

import jax
import jax.numpy as jnp
import numpy as np
from jax import lax
from jax.experimental import pallas as pl
from jax.experimental.pallas import tpu as pltpu

F32 = jnp.float32
BF16 = jnp.bfloat16
I32 = jnp.int32

EPS = 1e-6
N_MIXERS = 3
RET_HEADS = 4
RET_ROPE_BASE = 10000.0
GM_CHUNK = 128
GM_GROUPS = 4
GRID_W = 64
NA_HEADS = 32
NA_WIN_ROWS = 8
NA_WIN_COLS = 16
N_GROUPS = 4
EXPERTS_PER_GROUP = 4
N_EXPERTS = N_GROUPS * EXPERTS_PER_GROUP

LANES = 128
MXU_DIM = 256
VMEM_LIMIT_MIB = 56

RET_CHUNK = 256
RET_BLOCK = 1024
NA_ROWS_PER_STEP = 8
NA_HEAD_GROUP = 8
MOE_TILE = 256
ROUTER_TILE = 512
PAIRS = ((0, 1), (0, 2), (0, 3), (1, 2), (1, 3), (2, 3))
N_CLASSES = N_GROUPS * len(PAIRS)
CLASS_ROWS = 32
ROW_TILES = 8
HIGH16 = -65536
NEG_BIAS = -1e30


def _params(semantics, vmem_mib=VMEM_LIMIT_MIB):
    return pltpu.CompilerParams(dimension_semantics=semantics, vmem_limit_bytes=vmem_mib * 2**20)


def _silu(x):
    return x * jax.nn.sigmoid(x)


def _gelu_exact(x):
    return 0.5 * x * (1.0 + lax.erf(x * np.float32(np.sqrt(0.5))))


def _rows_to_tiles(src, dst, n):
    def body(blk, carry):
        r = pl.multiple_of(blk * 8, 8)
        for j in range(ROW_TILES):
            dst[pl.ds(blk * 8 * ROW_TILES + j, 8, stride=ROW_TILES), :] = src[pl.ds(r, 8), j * LANES:(j + 1) * LANES]
        return carry

    lax.fori_loop(0, n // 8, body, 0)


def _tiles_to_rows(src, dst, n):
    def body(blk, carry):
        r = pl.multiple_of(blk * 8, 8)
        for j in range(ROW_TILES):
            dst[pl.ds(r, 8), j * LANES:(j + 1) * LANES] = src[pl.ds(blk * 8 * ROW_TILES + j, 8, stride=ROW_TILES), :]
        return carry

    lax.fori_loop(0, n // 8, body, 0)


def _rms_modulate(x, g, sc, sh):
    y = x * lax.rsqrt(jnp.mean(x * x, axis=-1, keepdims=True) + EPS)
    return (y * g) * (1.0 + sc) + sh


def _ada_body(c_ref, w_ref, b_ref, o_ref):
    ca = _silu(c_ref[...]).astype(BF16)
    o_ref[0] = jnp.dot(ca, w_ref[0].astype(BF16), preferred_element_type=F32) + b_ref[0]


def _ada(c, w_ada, b_ada):
    depth, d, n = w_ada.shape
    bsz = c.shape[0]
    rows = 16
    tn = 1536
    c_pad = jnp.zeros((rows, d), F32).at[:bsz].set(c)
    out = pl.pallas_call(
        _ada_body,
        grid=(depth, n // tn),
        in_specs=[
            pl.BlockSpec((rows, d), lambda l, j: (0, 0)),
            pl.BlockSpec((1, d, tn), lambda l, j: (l, 0, j)),
            pl.BlockSpec((1, 1, tn), lambda l, j: (l, 0, j)),
        ],
        out_specs=pl.BlockSpec((1, rows, tn), lambda l, j: (l, 0, j)),
        out_shape=jax.ShapeDtypeStruct((depth, rows, n), F32),
        compiler_params=_params(("arbitrary", "arbitrary")),
        name="ada_mod",
    )(c_pad, w_ada, b_ada.reshape(depth, 1, n))
    return out[:, :bsz]


def _norm_mm_body(*refs, epi, tn, n_rot_tiles):
    if epi == "rot":
        x_ref, g_ref, sc_ref, sh_ref, w_ref, cs_ref, cos_ref, sin_ref, o_ref, h_scr = refs
    else:
        x_ref, g_ref, sc_ref, sh_ref, w_ref, cs_ref, o_ref, h_scr = refs
    j = pl.program_id(2)

    @pl.when(j == 0)
    def _():
        h_scr[...] = _rms_modulate(x_ref[0], g_ref[...], sc_ref[0], sh_ref[0]).astype(BF16)

    acc = jnp.dot(h_scr[...], w_ref[...], preferred_element_type=F32) * cs_ref[...]
    if epi == "gelu":
        o_ref[0] = _gelu_exact(acc).astype(o_ref.dtype)
    elif epi == "rot":
        @pl.when(j < n_rot_tiles)
        def _():
            cos = cos_ref[...]
            sin = sin_ref[...]
            for hh in range(tn // (2 * LANES)):
                lo = hh * 2 * LANES
                x1 = acc[:, lo:lo + LANES]
                x2 = acc[:, lo + LANES:lo + 2 * LANES]
                o_ref[0, :, lo:lo + LANES] = (x1 * cos - x2 * sin).astype(o_ref.dtype)
                o_ref[0, :, lo + LANES:lo + 2 * LANES] = (x1 * sin + x2 * cos).astype(o_ref.dtype)

        @pl.when(j >= n_rot_tiles)
        def _():
            o_ref[0] = acc.astype(o_ref.dtype)
    else:
        o_ref[0] = acc.astype(o_ref.dtype)


def _norm_mm(x, g, sc, sh, w, col_scale, *, epi="none", cos=None, sin=None, n_rot_cols=0, tm=1024, tn=1024):
    bsz, seq, d = x.shape
    n = w.shape[1]
    tm = min(tm, seq)
    in_specs = [
        pl.BlockSpec((1, tm, d), lambda b, i, j: (b, i, 0)),
        pl.BlockSpec((1, d), lambda b, i, j: (0, 0)),
        pl.BlockSpec((1, 1, d), lambda b, i, j: (b, 0, 0)),
        pl.BlockSpec((1, 1, d), lambda b, i, j: (b, 0, 0)),
        pl.BlockSpec((d, tn), lambda b, i, j: (0, j)),
        pl.BlockSpec((1, tn), lambda b, i, j: (0, j)),
    ]
    args = [x, g.reshape(1, d), sc, sh, w, col_scale.reshape(1, n)]
    if epi == "rot":
        in_specs += [pl.BlockSpec((tm, LANES), lambda b, i, j: (i, 0))] * 2
        args += [cos, sin]
    return pl.pallas_call(
        functools.partial(_norm_mm_body, epi=epi, tn=tn, n_rot_tiles=n_rot_cols // tn),
        grid=(bsz, seq // tm, n // tn),
        in_specs=in_specs,
        out_specs=pl.BlockSpec((1, tm, tn), lambda b, i, j: (b, i, j)),
        out_shape=jax.ShapeDtypeStruct((bsz, seq, n), BF16),
        scratch_shapes=[pltpu.VMEM((tm, d), BF16)],
        compiler_params=_params(("arbitrary", "arbitrary", "arbitrary")),
        name="norm_mm_" + epi,
    )(*args)


def _mm_res_body(a_ref, w_ref, x_ref, g_ref, o_ref):
    y = jnp.dot(a_ref[0], w_ref[...], preferred_element_type=F32)
    o_ref[0] = x_ref[0] + g_ref[0] * y


def _mm_res(a, w, x, gate, *, tm=512):
    bsz, seq, k = a.shape
    d = w.shape[1]
    tm = min(tm, seq)
    return pl.pallas_call(
        _mm_res_body,
        grid=(bsz, seq // tm),
        in_specs=[
            pl.BlockSpec((1, tm, k), lambda b, i: (b, i, 0)),
            pl.BlockSpec((k, d), lambda b, i: (0, 0)),
            pl.BlockSpec((1, tm, d), lambda b, i: (b, i, 0)),
            pl.BlockSpec((1, 1, d), lambda b, i: (b, 0, 0)),
        ],
        out_specs=pl.BlockSpec((1, tm, d), lambda b, i: (b, i, 0)),
        out_shape=jax.ShapeDtypeStruct((bsz, seq, d), F32),
        compiler_params=_params(("arbitrary", "arbitrary")),
        name="mm_res",
    )(a, w, x, gate)


def _ret_body(dl_ref, q_ref, k_ref, v_ref, gate_ref, o_ref, st_scr, cb_scr, *, nblk, tb, c):
    t = pl.program_id(2)
    nch = tb // c
    lg = jax.nn.log_sigmoid(dl_ref[0])
    lgf = lg[0:1, 0:1]
    lgb = lg[1:2, 0:1]
    idx = lax.broadcasted_iota(I32, (c, 1), 0).astype(F32)
    tn_dims = (((0,), (0,)), ((), ()))
    nt_dims = (((1,), (1,)), ((), ()))

    @pl.when((t == 0) | (t == nblk))
    def _():
        st_scr[...] = jnp.zeros_like(st_scr)

    @pl.when(t < nblk)
    def _():
        blk = nblk - 1 - t
        xi = jnp.exp(lgb * (c - idx))
        zeta = jnp.exp(lgb * idx)
        g_c = jnp.exp(lgb * c)
        for ci in reversed(range(nch)):
            rows = pl.ds(ci * c, c)
            q = q_ref[0, rows, :].astype(F32)
            k = k_ref[0, rows, :].astype(F32)
            v = v_ref[0, rows, :]
            st = st_scr[...]
            cross = jnp.dot((q * xi).astype(BF16), st.astype(BF16), preferred_element_type=F32)
            cb_scr[pl.ds(pl.multiple_of(blk * tb + ci * c, c), c), :] = cross.astype(BF16)
            st_scr[...] = st * g_c + lax.dot_general((k * zeta).astype(BF16), v, tn_dims,
                                                     preferred_element_type=F32)

    @pl.when(t >= nblk)
    def _():
        blk = t - nblk
        diff = (lax.broadcasted_iota(I32, (c, c), 0) - lax.broadcasted_iota(I32, (c, c), 1)).astype(F32)
        dmat = jnp.where(diff >= 0, jnp.exp(lgf * jnp.maximum(diff, 0.0)), jnp.exp(lgb * jnp.maximum(-diff, 0.0)))
        xi = jnp.exp(lgf * (idx + 1.0))
        zeta = jnp.exp(lgf * (c - 1.0 - idx))
        g_c = jnp.exp(lgf * c)
        for ci in range(nch):
            rows = pl.ds(ci * c, c)
            qb = q_ref[0, rows, :]
            kb = k_ref[0, rows, :]
            v = v_ref[0, rows, :]
            s = lax.dot_general(qb, kb, nt_dims, preferred_element_type=F32) * dmat
            o = jnp.dot(s.astype(BF16), v, preferred_element_type=F32)
            st = st_scr[...]
            o = o + jnp.dot((qb.astype(F32) * xi).astype(BF16), st.astype(BF16), preferred_element_type=F32)
            o = o + cb_scr[pl.ds(pl.multiple_of(blk * tb + ci * c, c), c), :].astype(F32)
            st_scr[...] = st * g_c + lax.dot_general((kb.astype(F32) * zeta).astype(BF16), v, tn_dims,
                                                     preferred_element_type=F32)
            o = o * lax.rsqrt(jnp.mean(o * o, axis=-1, keepdims=True) + EPS)
            o_ref[0, rows, :] = (_silu(gate_ref[0, rows, :].astype(F32)) * o).astype(o_ref.dtype)


def _retention_core(proj, decay_logit):
    bsz, seq, _ = proj.shape
    nh = RET_HEADS
    dk = proj.shape[2] // (6 * nh)
    dv = 2 * dk
    tb = min(RET_BLOCK, seq)
    c = min(RET_CHUNK, tb)
    nblk = seq // tb
    dl = jnp.broadcast_to(decay_logit.T[:, :, None].astype(F32), (nh, 2, LANES))

    def blk_of(t):
        return jnp.where(t < nblk, nblk - 1 - t, t - nblk)

    def fwd_blk(t):
        return jnp.where(t < nblk, 0, t - nblk)

    return pl.pallas_call(
        functools.partial(_ret_body, nblk=nblk, tb=tb, c=c),
        grid=(bsz, nh, 2 * nblk),
        in_specs=[
            pl.BlockSpec((1, 2, LANES), lambda b, h, t: (h, 0, 0)),
            pl.BlockSpec((1, tb, dk), lambda b, h, t: (b, blk_of(t), h)),
            pl.BlockSpec((1, tb, dk), lambda b, h, t: (b, blk_of(t), nh + h)),
            pl.BlockSpec((1, tb, dv), lambda b, h, t: (b, blk_of(t), nh + h)),
            pl.BlockSpec((1, tb, dv), lambda b, h, t: (b, fwd_blk(t), 2 * nh + h)),
        ],
        out_specs=pl.BlockSpec((1, tb, dv), lambda b, h, t: (b, fwd_blk(t), h)),
        out_shape=jax.ShapeDtypeStruct((bsz, seq, nh * dv), BF16),
        scratch_shapes=[pltpu.VMEM((dk, dv), F32), pltpu.VMEM((seq, dv), BF16)],
        compiler_params=_params(("arbitrary", "arbitrary", "arbitrary")),
        name="retention_core",
    )(dl, proj, proj, proj, proj)


def _gmlp_body(u_ref, v_ref, lng_ref, lnb_ref, ws_ref, bs_ref, wo_ref, x_ref, g_ref, o_ref, a_scr, *, tb, gd):
    v = v_ref[0].astype(F32)
    mu = jnp.mean(v, axis=-1, keepdims=True)
    vc = v - mu
    var = jnp.mean(vc * vc, axis=-1, keepdims=True)
    vn = ((vc * lax.rsqrt(var + EPS)) * lng_ref[...] + lnb_ref[...]).astype(BF16)
    for ci in range(tb // GM_CHUNK):
        r0 = ci * GM_CHUNK
        for gi in range(GM_GROUPS):
            c0 = gi * gd
            mixed = jnp.dot(ws_ref[gi], vn[r0:r0 + GM_CHUNK, c0:c0 + gd], preferred_element_type=F32) + bs_ref[gi]
            u = u_ref[0, r0:r0 + GM_CHUNK, c0:c0 + gd].astype(F32)
            a_scr[r0:r0 + GM_CHUNK, c0:c0 + gd] = (u * mixed).astype(BF16)
    y = jnp.dot(a_scr[...], wo_ref[...], preferred_element_type=F32)
    o_ref[0] = x_ref[0] + g_ref[0] * y


def _gmlp_core(z, ln_g, ln_b, w_s, b_s, w_out, x, gate, *, tb=256):
    bsz, seq, ffn = z.shape
    half = ffn // 2
    gd = half // GM_GROUPS
    d = w_out.shape[1]
    tb = min(tb, seq)
    return pl.pallas_call(
        functools.partial(_gmlp_body, tb=tb, gd=gd),
        grid=(bsz, seq // tb),
        in_specs=[
            pl.BlockSpec((1, tb, half), lambda b, i: (b, i, 0)),
            pl.BlockSpec((1, tb, half), lambda b, i: (b, i, 1)),
            pl.BlockSpec((1, half), lambda b, i: (0, 0)),
            pl.BlockSpec((1, half), lambda b, i: (0, 0)),
            pl.BlockSpec((GM_GROUPS, GM_CHUNK, GM_CHUNK), lambda b, i: (0, 0, 0)),
            pl.BlockSpec((GM_GROUPS, GM_CHUNK, 1), lambda b, i: (0, 0, 0)),
            pl.BlockSpec((half, d), lambda b, i: (0, 0)),
            pl.BlockSpec((1, tb, d), lambda b, i: (b, i, 0)),
            pl.BlockSpec((1, 1, d), lambda b, i: (b, 0, 0)),
        ],
        out_specs=pl.BlockSpec((1, tb, d), lambda b, i: (b, i, 0)),
        out_shape=jax.ShapeDtypeStruct((bsz, seq, d), F32),
        scratch_shapes=[pltpu.VMEM((tb, half), BF16)],
        compiler_params=_params(("arbitrary", "arbitrary")),
        name="gmlp_core",
    )(z, z, ln_g.reshape(1, half), ln_b.reshape(1, half), w_s.astype(BF16),
      b_s.reshape(GM_GROUPS, GM_CHUNK, 1), w_out, x, gate)


def _na_body(q_ref, kp_ref, kc_ref, kn_ref, vp_ref, vc_ref, vn_ref, bias_ref, o_ref, kw_scr, vw_scr,
             *, n_rows, rps, gw, hg, dh):
    i = pl.program_id(2)
    kw_scr[0:rps] = kp_ref[0]
    kw_scr[rps:2 * rps] = kc_ref[0]
    kw_scr[2 * rps:3 * rps] = kn_ref[0]
    vw_scr[0:rps] = vp_ref[0]
    vw_scr[rps:2 * rps] = vc_ref[0]
    vw_scr[2 * rps:3 * rps] = vn_ref[0]
    ch = hg * dh
    nkeys = NA_WIN_ROWS * gw
    head_of_ch = lax.broadcasted_iota(I32, (gw, ch), 1) // dh
    nt_dims = (((1,), (1,)), ((), ()))
    for rl in range(rps):
        r = i * rps + rl
        r0 = jnp.clip(r - NA_WIN_ROWS // 2, 0, n_rows - NA_WIN_ROWS)
        l0 = r0 - i * rps + rps
        ro0 = r0 - r + NA_WIN_ROWS - 1
        q = q_ref[0, rl]
        zero = jnp.zeros_like(q)
        qm = jnp.concatenate([jnp.where(head_of_ch == hh, q, zero) for hh in range(hg)], axis=0)
        kwin = kw_scr[pl.ds(l0, NA_WIN_ROWS)].reshape(nkeys, ch)
        vwin = vw_scr[pl.ds(l0, NA_WIN_ROWS)].reshape(nkeys, ch)
        s = lax.dot_general(qm, kwin, nt_dims, preferred_element_type=F32)
        s = s + jnp.concatenate([bias_ref[ro0 + 2 * jj, 0] for jj in range(NA_WIN_ROWS // 2)], axis=1)
        m = jnp.max(s, axis=-1, keepdims=True)
        e = jnp.exp(s - m)
        p = e / jnp.sum(e, axis=-1, keepdims=True)
        pv = jnp.dot(p.astype(BF16), vwin, preferred_element_type=F32)
        out = jnp.zeros((gw, ch), F32)
        for hh in range(hg):
            out = out + jnp.where(head_of_ch == hh, pv[hh * gw:(hh + 1) * gw], 0.0)
        o_ref[0, rl] = out.astype(o_ref.dtype)


def _na_bias_table(rpb, gw):
    nh = rpb.shape[0]
    kc = NA_WIN_COLS
    col = jnp.arange(gw)
    c0 = jnp.clip(col - kc // 2, 0, gw - kc)
    key = jnp.arange(gw)
    rel = key[None, :] - col[:, None] + kc - 1
    inside = (key[None, :] >= c0[:, None]) & (key[None, :] < c0[:, None] + kc)
    dense = jnp.where(inside[None, None], rpb[:, :, jnp.clip(rel, 0, 2 * kc - 2)].astype(F32), NEG_BIAS)
    pair = jnp.concatenate([dense[:, :-1], dense[:, 1:]], axis=-1)
    pair = pair.reshape(nh // NA_HEAD_GROUP, NA_HEAD_GROUP, 2 * NA_WIN_ROWS - 2, gw, 2 * gw)
    return jnp.transpose(pair, (2, 0, 1, 3, 4)).reshape(2 * NA_WIN_ROWS - 2, nh // NA_HEAD_GROUP,
                                                       NA_HEAD_GROUP * gw, 2 * gw)


def _na_core(qkv, rpb):
    bsz, seq, d3 = qkv.shape
    d = d3 // 3
    gw = GRID_W
    n_rows = seq // gw
    rps = NA_ROWS_PER_STEP
    hg = NA_HEAD_GROUP
    dh = d // NA_HEADS
    ch = hg * dh
    ngrp = NA_HEADS // hg
    nb = n_rows // rps
    qkv4 = qkv.reshape(bsz, n_rows, gw, d3)
    bias = _na_bias_table(rpb, gw)
    n_pairs = bias.shape[0]

    def prev(i):
        return jnp.maximum(i - 1, 0)

    def nxt(i):
        return jnp.minimum(i + 1, nb - 1)

    def spec(row_fn, col_off):
        return pl.BlockSpec((1, rps, gw, ch), lambda b, g, i: (b, row_fn(i), 0, col_off + g))

    out = pl.pallas_call(
        functools.partial(_na_body, n_rows=n_rows, rps=rps, gw=gw, hg=hg, dh=dh),
        grid=(bsz, ngrp, nb),
        in_specs=[
            spec(lambda i: i, 0),
            spec(prev, ngrp), spec(lambda i: i, ngrp), spec(nxt, ngrp),
            spec(prev, 2 * ngrp), spec(lambda i: i, 2 * ngrp), spec(nxt, 2 * ngrp),
            pl.BlockSpec((n_pairs, 1, hg * gw, 2 * gw), lambda b, g, i: (0, g, 0, 0)),
        ],
        out_specs=pl.BlockSpec((1, rps, gw, ch), lambda b, g, i: (b, i, 0, g)),
        out_shape=jax.ShapeDtypeStruct((bsz, n_rows, gw, d), BF16),
        scratch_shapes=[pltpu.VMEM((3 * rps, gw, ch), BF16), pltpu.VMEM((3 * rps, gw, ch), BF16)],
        compiler_params=_params(("arbitrary", "arbitrary", "arbitrary")),
        name="na_core",
    )(qkv4, qkv4, qkv4, qkv4, qkv4, qkv4, qkv4, bias)
    return out.reshape(bsz, seq, d)


def _router_body(x_ref, g_ref, sc_ref, sh_ref, rwt_ref, rb_ref, hrow_ref, meta_ref, tot_ref, run_scr, h_scr, *, tm):
    first = (pl.program_id(0) == 0) & (pl.program_id(1) == 0)

    @pl.when(first)
    def _():
        run_scr[...] = jnp.zeros_like(run_scr)

    h = _rms_modulate(x_ref[0], g_ref[...], sc_ref[0], sh_ref[0])
    nt_dims = (((1,), (1,)), ((), ()))
    logits = lax.dot_general(rwt_ref[...], h, nt_dims, preferred_element_type=F32,
                             precision=lax.Precision.HIGHEST)
    scores = jax.nn.sigmoid(logits)
    biased = scores + rb_ref[...]
    b = [biased[e:e + 1, :] for e in range(N_EXPERTS)]
    sc = [scores[e:e + 1, :] for e in range(N_EXPERTS)]

    best = None
    grp = jnp.zeros((1, tm), I32)
    for gi in range(N_GROUPS):
        gs = None
        for (pi, pj) in PAIRS:
            ps = b[4 * gi + pi] + b[4 * gi + pj]
            gs = ps if gs is None else jnp.maximum(gs, ps)
        if best is None:
            best = gs
        else:
            better = gs > best
            grp = jnp.where(better, gi, grp)
            best = jnp.where(better, gs, best)

    def pick(vals, kk):
        out = vals[kk]
        for gi in range(1, N_GROUPS):
            out = jnp.where(grp == gi, vals[4 * gi + kk], out)
        return out

    bsel = [pick(b, kk) for kk in range(EXPERTS_PER_GROUP)]
    ssel = [pick(sc, kk) for kk in range(EXPERTS_PER_GROUP)]
    chosen = []
    for kk in range(EXPERTS_PER_GROUP):
        beaten = jnp.zeros((1, tm), I32)
        for jj in range(EXPERTS_PER_GROUP):
            if jj == kk:
                continue
            wins = (bsel[jj] > bsel[kk]) | ((bsel[jj] == bsel[kk]) & (jj < kk))
            beaten = beaten + wins.astype(I32)
        chosen.append(beaten < 2)

    pid = jnp.zeros((1, tm), I32)
    s_lo = jnp.zeros((1, tm), F32)
    s_hi = jnp.zeros((1, tm), F32)
    for pidx, (pi, pj) in enumerate(PAIRS):
        hit = chosen[pi] & chosen[pj]
        pid = jnp.where(hit, pidx, pid)
        s_lo = jnp.where(hit, ssel[pi], s_lo)
        s_hi = jnp.where(hit, ssel[pj], s_hi)
    denom = s_lo + s_hi
    w_lo = s_lo / denom
    w_hi = s_hi / denom
    cls = grp * len(PAIRS) + pid

    onehot = (lax.broadcasted_iota(I32, (CLASS_ROWS, tm), 0) == cls).astype(F32)
    tri = (lax.broadcasted_iota(I32, (tm, tm), 0) < lax.broadcasted_iota(I32, (tm, tm), 1)).astype(BF16)
    before = jnp.dot(onehot.astype(BF16), tri, preferred_element_type=F32)
    run = run_scr[...]
    rank = jnp.sum(onehot * (before + run[:, 0:1]), axis=0, keepdims=True)
    run = run + jnp.sum(onehot, axis=1, keepdims=True)
    run_scr[...] = run
    tot_ref[...] = run

    meta = jnp.concatenate([cls.astype(F32), rank, w_lo, w_hi, w_lo, w_hi, jnp.zeros((2, tm), F32)], axis=0)
    meta_ref[...] = meta
    meta_t = jnp.concatenate([meta, jnp.zeros((LANES - 8, tm), F32)], axis=0).T

    lane = lax.broadcasted_iota(I32, (tm, LANES), 1)
    mbits = lax.bitcast_convert_type(meta_t, I32)
    payload = jnp.where((lane == 2) | (lane == 3), lax.shift_right_logical(mbits, 16),
                        jnp.where((lane == 4) | (lane == 5), mbits & 0xFFFF, 0))
    h0 = lax.bitcast_convert_type(h[:, 0:LANES].astype(BF16).astype(F32), I32)
    h_scr[:, 0:LANES] = lax.bitcast_convert_type((h0 & HIGH16) | payload, F32)
    h_scr[:, LANES:] = h[:, LANES:]
    _rows_to_tiles(h_scr, hrow_ref, tm)


def _router(x, g, sc, sh, router_w, router_b):
    bsz, seq, d = x.shape
    tm = min(ROUTER_TILE, seq)
    nt = seq // tm
    return pl.pallas_call(
        functools.partial(_router_body, tm=tm),
        grid=(bsz, nt),
        in_specs=[
            pl.BlockSpec((1, tm, d), lambda b, i: (b, i, 0)),
            pl.BlockSpec((1, d), lambda b, i: (0, 0)),
            pl.BlockSpec((1, 1, d), lambda b, i: (b, 0, 0)),
            pl.BlockSpec((1, 1, d), lambda b, i: (b, 0, 0)),
            pl.BlockSpec((N_EXPERTS, d), lambda b, i: (0, 0)),
            pl.BlockSpec((N_EXPERTS, 1), lambda b, i: (0, 0)),
        ],
        out_specs=[
            pl.BlockSpec((tm * ROW_TILES, LANES), lambda b, i: (b * nt + i, 0)),
            pl.BlockSpec((8, tm), lambda b, i: (0, b * nt + i)),
            pl.BlockSpec((CLASS_ROWS, LANES), lambda b, i: (0, 0)),
        ],
        out_shape=[
            jax.ShapeDtypeStruct((bsz * seq * ROW_TILES, LANES), F32),
            jax.ShapeDtypeStruct((8, bsz * seq), F32),
            jax.ShapeDtypeStruct((CLASS_ROWS, LANES), F32),
        ],
        scratch_shapes=[pltpu.VMEM((CLASS_ROWS, LANES), F32), pltpu.VMEM((tm, d), F32)],
        compiler_params=_params(("arbitrary", "arbitrary")),
        name="moe_router",
    )(x, g.reshape(1, d), sc, sh, router_w.T, router_b.reshape(N_EXPERTS, 1))


def _token_copy(src, src_tok, dst, dst_tok, sem):
    s = pl.multiple_of(src_tok * ROW_TILES, ROW_TILES)
    t = pl.multiple_of(dst_tok * ROW_TILES, ROW_TILES)
    return pltpu.make_async_copy(src.at[pl.ds(s, ROW_TILES)], dst.at[pl.ds(t, ROW_TILES)], sem)


def _dispatch_body(slot_ref, h_ref, init_ref, o_ref, sem, *, tb):
    del init_ref
    base = pl.program_id(0) * tb

    def start(i, carry):
        _token_copy(h_ref, base + i, o_ref, slot_ref[i], sem).start()
        return carry

    def wait(i, carry):
        _token_copy(h_ref, base + i, o_ref, slot_ref[i], sem).wait()
        return carry

    lax.fori_loop(0, tb, start, 0, unroll=8)
    lax.fori_loop(0, tb, wait, 0, unroll=8)


def _dispatch(hrow, slot, n_slots, *, tb=2048):
    n_tok = hrow.shape[0] // ROW_TILES
    tb = min(tb, n_tok)
    return pl.pallas_call(
        functools.partial(_dispatch_body, tb=tb),
        grid=(n_tok // tb,),
        in_specs=[
            pl.BlockSpec((tb,), lambda i: (i,), memory_space=pltpu.SMEM),
            pl.BlockSpec(memory_space=pl.ANY),
            pl.BlockSpec(memory_space=pl.ANY),
        ],
        out_specs=pl.BlockSpec(memory_space=pl.ANY),
        out_shape=jax.ShapeDtypeStruct((n_slots * ROW_TILES, LANES), hrow.dtype),
        scratch_shapes=[pltpu.SemaphoreType.DMA(())],
        input_output_aliases={2: 0},
        compiler_params=_params(("arbitrary",)),
        name="moe_dispatch",
    )(slot, hrow, jnp.zeros((n_slots * ROW_TILES, LANES), hrow.dtype))


def _expert_body(e0_ref, e1_ref, valid_ref, hs_ref, wgu0_ref, wgu1_ref, wd0_ref, wd1_ref, o_ref, h_scr, y_scr,
                 *, tile, de):
    del e0_ref, e1_ref
    i = pl.program_id(0)

    @pl.when(valid_ref[i] == 1)
    def _():
        _tiles_to_rows(hs_ref, h_scr, tile)
        w0 = lax.bitcast_convert_type(h_scr[:, 0:LANES], I32)
        pay = w0 & 0xFFFF
        w_lo = lax.bitcast_convert_type(lax.shift_left(pay[:, 2:3], 16) | pay[:, 4:5], F32)
        w_hi = lax.bitcast_convert_type(lax.shift_left(pay[:, 3:4], 16) | pay[:, 5:6], F32)
        h_scr[:, 0:LANES] = lax.bitcast_convert_type(w0 & HIGH16, F32)
        h = h_scr[...].astype(BF16)
        gu0 = jnp.dot(h, wgu0_ref[0], preferred_element_type=F32)
        hid0 = (_silu(gu0[:, 0:de]) * gu0[:, de:2 * de]) * w_lo
        gu1 = jnp.dot(h, wgu1_ref[0], preferred_element_type=F32)
        hid1 = (_silu(gu1[:, 0:de]) * gu1[:, de:2 * de]) * w_hi
        y = jnp.dot(hid0.astype(BF16), wd0_ref[0], preferred_element_type=F32)
        y_scr[...] = y + jnp.dot(hid1.astype(BF16), wd1_ref[0], preferred_element_type=F32)
        _rows_to_tiles(y_scr, o_ref, tile)

    @pl.when(valid_ref[i] == 0)
    def _():
        o_ref[...] = jnp.zeros_like(o_ref)


def _experts(hs, tile_e0, tile_e1, tile_valid, w_gate_up, w_down):
    d = ROW_TILES * LANES
    de = w_down.shape[1]
    n_tiles = hs.shape[0] // (MOE_TILE * ROW_TILES)
    blk = (MOE_TILE * ROW_TILES, LANES)
    grid_spec = pltpu.PrefetchScalarGridSpec(
        num_scalar_prefetch=3,
        grid=(n_tiles,),
        in_specs=[
            pl.BlockSpec(blk, lambda i, e0, e1, va: (i, 0)),
            pl.BlockSpec((1, d, 2 * de), lambda i, e0, e1, va: (e0[i], 0, 0)),
            pl.BlockSpec((1, d, 2 * de), lambda i, e0, e1, va: (e1[i], 0, 0)),
            pl.BlockSpec((1, de, d), lambda i, e0, e1, va: (e0[i], 0, 0)),
            pl.BlockSpec((1, de, d), lambda i, e0, e1, va: (e1[i], 0, 0)),
        ],
        out_specs=pl.BlockSpec(blk, lambda i, e0, e1, va: (i, 0)),
        scratch_shapes=[pltpu.VMEM((MOE_TILE, d), F32), pltpu.VMEM((MOE_TILE, d), F32)],
    )
    return pl.pallas_call(
        functools.partial(_expert_body, tile=MOE_TILE, de=de),
        grid_spec=grid_spec,
        out_shape=jax.ShapeDtypeStruct(hs.shape, F32),
        compiler_params=_params(("arbitrary",)),
        name="moe_experts",
    )(tile_e0, tile_e1, tile_valid, hs, w_gate_up, w_gate_up, w_down, w_down)


def _combine_body(slot_ref, ys_ref, x_ref, g_ref, o_ref, ybuf, y_scr, sem, *, tb):
    def start(i, carry):
        _token_copy(ys_ref, slot_ref[i], ybuf, i, sem).start()
        return carry

    def wait(i, carry):
        _token_copy(ys_ref, slot_ref[i], ybuf, i, sem).wait()
        return carry

    lax.fori_loop(0, tb, start, 0, unroll=8)
    lax.fori_loop(0, tb, wait, 0, unroll=8)
    _tiles_to_rows(ybuf, y_scr, tb)
    o_ref[0] = x_ref[0] + g_ref[0] * y_scr[...]


def _combine(ys, slot, x, gate, *, tb=256):
    bsz, seq, d = x.shape
    tb = min(tb, seq)
    nt = seq // tb
    return pl.pallas_call(
        functools.partial(_combine_body, tb=tb),
        grid=(bsz, nt),
        in_specs=[
            pl.BlockSpec((tb,), lambda b, i: (b * nt + i,), memory_space=pltpu.SMEM),
            pl.BlockSpec(memory_space=pl.ANY),
            pl.BlockSpec((1, tb, d), lambda b, i: (b, i, 0)),
            pl.BlockSpec((1, 1, d), lambda b, i: (b, 0, 0)),
        ],
        out_specs=pl.BlockSpec((1, tb, d), lambda b, i: (b, i, 0)),
        out_shape=jax.ShapeDtypeStruct((bsz, seq, d), F32),
        scratch_shapes=[pltpu.VMEM((tb * ROW_TILES, LANES), F32), pltpu.VMEM((tb, d), F32),
                        pltpu.SemaphoreType.DMA(())],
        compiler_params=_params(("arbitrary", "arbitrary")),
        name="moe_combine",
    )(slot, ys, x, gate)


_PAIR_LO = np.array([p[0] for p in PAIRS], np.int32)
_PAIR_HI = np.array([p[1] for p in PAIRS], np.int32)


def _moe(x, g, sc, sh, gate, router_w, router_b, w_gate_up, w_down):
    bsz, seq, d = x.shape
    n_tok = bsz * seq
    hrow, meta, tot = _router(x, g, sc, sh, router_w, router_b)

    counts = tot[:N_CLASSES, 0].astype(I32)
    tiles_per_class = (counts + MOE_TILE - 1) // MOE_TILE
    tile_end = jnp.cumsum(tiles_per_class)
    tile_start = tile_end - tiles_per_class
    cls = meta[0].astype(I32)
    rank = meta[1].astype(I32)
    cls_onehot = cls[:, None] == jnp.arange(N_CLASSES, dtype=I32)[None, :]
    slot = jnp.sum(jnp.where(cls_onehot, tile_start[None, :] * MOE_TILE, 0), axis=1) + rank

    n_tiles = n_tok // MOE_TILE + N_CLASSES
    n_slots = n_tiles * MOE_TILE
    tile_id = jnp.arange(n_tiles, dtype=I32)
    tile_valid = (tile_id < tile_end[-1]).astype(I32)
    last_valid = jnp.maximum(tile_end[-1] - 1, 0)
    tile_cls = jnp.sum((jnp.minimum(tile_id, last_valid)[:, None] >= tile_end[None, :]).astype(I32), axis=1)
    tile_cls = jnp.minimum(tile_cls, N_CLASSES - 1)
    grp = tile_cls // len(PAIRS)
    pair = tile_cls % len(PAIRS)
    tile_e0 = grp * EXPERTS_PER_GROUP + jnp.asarray(_PAIR_LO)[pair]
    tile_e1 = grp * EXPERTS_PER_GROUP + jnp.asarray(_PAIR_HI)[pair]

    hs = _dispatch(hrow, slot, n_slots)
    ys = _experts(hs, tile_e0, tile_e1, tile_valid, w_gate_up, w_down)
    return _combine(ys, slot, x, gate)


def _rotary_tables(seq, half):
    inv_freq = RET_ROPE_BASE ** (-jnp.arange(half, dtype=F32) / half)
    ang = jnp.arange(seq, dtype=F32)[:, None] * inv_freq[None, :]
    return jnp.cos(ang), jnp.sin(ang)


def _retention_layer(x, g, sc, sh, gate, w_in, decay_logit, w_out):
    seq = x.shape[1]
    n = w_in.shape[1]
    nq = n // 6
    dk = nq // RET_HEADS
    cos, sin = _rotary_tables(seq, dk // 2)
    col_scale = jnp.concatenate([jnp.ones((nq,), F32), jnp.full((nq,), dk ** -0.5, F32), jnp.ones((n - 2 * nq,), F32)])
    proj = _norm_mm(x, g, sc, sh, w_in.astype(BF16), col_scale, epi="rot", cos=cos, sin=sin, n_rot_cols=2 * nq)
    a = _retention_core(proj, decay_logit)
    return _mm_res(a, w_out.astype(BF16), x, gate)


def _gmlp_layer(x, g, sc, sh, gate, w_in, ln_g, ln_b, w_s, b_s, w_out):
    n = w_in.shape[1]
    z = _norm_mm(x, g, sc, sh, w_in.astype(BF16), jnp.ones((n,), F32), epi="gelu")
    return _gmlp_core(z, ln_g, ln_b, w_s, b_s, w_out.astype(BF16), x, gate)


def _na_layer(x, g, sc, sh, gate, w_qkv, rpb, w_out):
    d = x.shape[2]
    dh = d // NA_HEADS
    col_scale = jnp.concatenate([jnp.full((d,), dh ** -0.5, F32), jnp.ones((2 * d,), F32)])
    qkv = _norm_mm(x, g, sc, sh, w_qkv.astype(BF16), col_scale)
    a = _na_core(qkv, rpb)
    return _mm_res(a, w_out.astype(BF16), x, gate)


def _final_norm_body(x_ref, g_ref, o_ref):
    x = x_ref[0]
    o_ref[0] = (x * lax.rsqrt(jnp.mean(x * x, axis=-1, keepdims=True) + EPS)) * g_ref[...]


def _final_norm(x, g, *, tm=1024):
    bsz, seq, d = x.shape
    tm = min(tm, seq)
    return pl.pallas_call(
        _final_norm_body,
        grid=(bsz, seq // tm),
        in_specs=[pl.BlockSpec((1, tm, d), lambda b, i: (b, i, 0)), pl.BlockSpec((1, d), lambda b, i: (0, 0))],
        out_specs=pl.BlockSpec((1, tm, d), lambda b, i: (b, i, 0)),
        out_shape=jax.ShapeDtypeStruct((bsz, seq, d), F32),
        compiler_params=_params(("arbitrary", "arbitrary")),
        name="final_norm",
    )(x, g.reshape(1, d))


def kernel(x, c, norm_g, final_g, w_ada, b_ada, ret_w_in, ret_decay_logit, ret_w_out, gm_w_in, gm_ln_g, gm_ln_b,
           gm_w_s, gm_b_s, gm_w_out, na_w_qkv, na_rpb, na_w_out, router_w, router_b, moe_w_gate_up, moe_w_down):
    depth = w_ada.shape[0]
    bsz, _, d = x.shape
    mod = _ada(c, w_ada, b_ada)
    for i in range(depth):
        kind, j = i % N_MIXERS, i // N_MIXERS
        sh1, sc1, g1, sh2, sc2, g2 = [mod[i, :, k * d:(k + 1) * d].reshape(bsz, 1, d) for k in range(6)]
        if kind == 0:
            x = _retention_layer(x, norm_g[i, 0], sc1, sh1, g1, ret_w_in[j], ret_decay_logit[j], ret_w_out[j])
        elif kind == 1:
            x = _gmlp_layer(x, norm_g[i, 0], sc1, sh1, g1, gm_w_in[j], gm_ln_g[j], gm_ln_b[j], gm_w_s[j],
                            gm_b_s[j], gm_w_out[j])
        else:
            x = _na_layer(x, norm_g[i, 0], sc1, sh1, g1, na_w_qkv[j], na_rpb[j], na_w_out[j])
        x = _moe(x, norm_g[i, 1], sc2, sh2, g2, router_w, router_b,
                 moe_w_gate_up[i].astype(BF16), moe_w_down[i].astype(BF16))
    return _final_norm(x, final_g)
```

```python
import functools

import jax
import jax.numpy as jnp
import numpy as np
from jax import lax
from jax.experimental import pallas as pl
from jax.experimental.pallas import tpu as pltpu

F32 = jnp.float32
BF16 = jnp.bfloat16
I32 = jnp.int32

EPS = 1e-6
N_MIXERS = 3
RET_HEADS = 4
RET_ROPE_BASE = 10000.0
GM_CHUNK = 128
GM_GROUPS = 4
GRID_W = 64
NA_HEADS = 32
NA_WIN_ROWS = 8
NA_WIN_COLS = 16
N_GROUPS = 4
EXPERTS_PER_GROUP = 4
N_EXPERTS = N_GROUPS * EXPERTS_PER_GROUP

LANES = 128
MXU_DIM = 256
VMEM_LIMIT_MIB = 56

RET_CHUNK = 256
RET_BLOCK = 1024
NA_ROWS_PER_STEP = 8
NA_HEAD_GROUP = 8
MOE_TILE = 256
ROUTER_TILE = 512
PAIRS = ((0, 1), (0, 2), (0, 3), (1, 2), (1, 3), (2, 3))
N_CLASSES = N_GROUPS * len(PAIRS)
CLASS_ROWS = 32
ROW_TILES = 8
HIGH16 = -65536
NEG_BIAS = -1e30
LOG2_E = float(np.log2(np.e))


def _params(semantics, vmem_mib=VMEM_LIMIT_MIB):
    return pltpu.CompilerParams(dimension_semantics=semantics, vmem_limit_bytes=vmem_mib * 2**20)


def _silu(x):
    return x * jax.nn.sigmoid(x)


def _gelu_exact(x):
    return 0.5 * x * (1.0 + lax.erf(x * np.float32(np.sqrt(0.5))))


def _token_cols(ref, j, n):
    return ref[pl.ds(j, n, stride=ROW_TILES), :]


def _store_token_tiles(dst, val, n):
    for j in range(ROW_TILES):
        dst[pl.ds(j, n, stride=ROW_TILES), :] = val[:, j * LANES:(j + 1) * LANES]


def _load_token_tiles(src, n):
    return jnp.concatenate([_token_cols(src, j, n) for j in range(ROW_TILES)], axis=1)


def _rms_modulate(x, g, sc, sh):
    y = x * lax.rsqrt(jnp.mean(x * x, axis=-1, keepdims=True) + EPS)
    return (y * g) * (1.0 + sc) + sh


def _ada_body(c_ref, w_ref, b_ref, o_ref):
    ca = _silu(c_ref[...]).astype(BF16)
    o_ref[0] = jnp.dot(ca, w_ref[0].astype(BF16), preferred_element_type=F32) + b_ref[0]


def _ada(c, w_ada, b_ada):
    depth, d, n = w_ada.shape
    bsz = c.shape[0]
    rows = 16
    tn = 1536
    c_pad = jnp.zeros((rows, d), F32).at[:bsz].set(c)
    out = pl.pallas_call(
        _ada_body,
        grid=(depth, n // tn),
        in_specs=[
            pl.BlockSpec((rows, d), lambda l, j: (0, 0)),
            pl.BlockSpec((1, d, tn), lambda l, j: (l, 0, j)),
            pl.BlockSpec((1, 1, tn), lambda l, j: (l, 0, j)),
        ],
        out_specs=pl.BlockSpec((1, rows, tn), lambda l, j: (l, 0, j)),
        out_shape=jax.ShapeDtypeStruct((depth, rows, n), F32),
        compiler_params=_params(("arbitrary", "arbitrary")),
        name="ada_mod",
    )(c_pad, w_ada, b_ada.reshape(depth, 1, n))
    return out[:, :bsz]


def _norm_mm_body(*refs, epi, tn, n_rot_tiles):
    if epi == "rot":
        x_ref, g_ref, sc_ref, sh_ref, w_ref, cs_ref, cos_ref, sin_ref, o_ref, h_scr = refs
    else:
        x_ref, g_ref, sc_ref, sh_ref, w_ref, cs_ref, o_ref, h_scr = refs
    j = pl.program_id(2)

    @pl.when(j == 0)
    def _():
        h_scr[...] = _rms_modulate(x_ref[0], g_ref[...], sc_ref[0], sh_ref[0]).astype(BF16)

    acc = jnp.dot(h_scr[...], w_ref[...], preferred_element_type=F32) * cs_ref[...]
    if epi == "gelu":
        o_ref[0] = _gelu_exact(acc).astype(o_ref.dtype)
    elif epi == "rot":
        @pl.when(j < n_rot_tiles)
        def _():
            cos = cos_ref[...]
            sin = sin_ref[...]
            for hh in range(tn // (2 * LANES)):
                lo = hh * 2 * LANES
                x1 = acc[:, lo:lo + LANES]
                x2 = acc[:, lo + LANES:lo + 2 * LANES]
                o_ref[0, :, lo:lo + LANES] = (x1 * cos - x2 * sin).astype(o_ref.dtype)
                o_ref[0, :, lo + LANES:lo + 2 * LANES] = (x1 * sin + x2 * cos).astype(o_ref.dtype)

        @pl.when(j >= n_rot_tiles)
        def _():
            o_ref[0] = acc.astype(o_ref.dtype)
    else:
        o_ref[0] = acc.astype(o_ref.dtype)


def _norm_mm(x, g, sc, sh, w, col_scale, *, epi="none", cos=None, sin=None, n_rot_cols=0, tm=1024, tn=1024):
    bsz, seq, d = x.shape
    n = w.shape[1]
    tm = min(tm, seq)
    in_specs = [
        pl.BlockSpec((1, tm, d), lambda b, i, j: (b, i, 0)),
        pl.BlockSpec((1, d), lambda b, i, j: (0, 0)),
        pl.BlockSpec((1, 1, d), lambda b, i, j: (b, 0, 0)),
        pl.BlockSpec((1, 1, d), lambda b, i, j: (b, 0, 0)),
        pl.BlockSpec((d, tn), lambda b, i, j: (0, j)),
        pl.BlockSpec((1, tn), lambda b, i, j: (0, j)),
    ]
    args = [x, g.reshape(1, d), sc, sh, w, col_scale.reshape(1, n)]
    if epi == "rot":
        in_specs += [pl.BlockSpec((tm, LANES), lambda b, i, j: (i, 0))] * 2
        args += [cos, sin]
    return pl.pallas_call(
        functools.partial(_norm_mm_body, epi=epi, tn=tn, n_rot_tiles=n_rot_cols // tn),
        grid=(bsz, seq // tm, n // tn),
        in_specs=in_specs,
        out_specs=pl.BlockSpec((1, tm, tn), lambda b, i, j: (b, i, j)),
        out_shape=jax.ShapeDtypeStruct((bsz, seq, n), BF16),
        scratch_shapes=[pltpu.VMEM((tm, d), BF16)],
        compiler_params=_params(("arbitrary", "arbitrary", "arbitrary")),
        name="norm_mm_" + epi,
    )(*args)


def _mm_res_body(a_ref, w_ref, x_ref, g_ref, o_ref):
    y = jnp.dot(a_ref[0], w_ref[...], preferred_element_type=F32)
    o_ref[0] = x_ref[0] + g_ref[0] * y


def _mm_res(a, w, x, gate, *, tm=512):
    bsz, seq, k = a.shape
    d = w.shape[1]
    tm = min(tm, seq)
    return pl.pallas_call(
        _mm_res_body,
        grid=(bsz, seq // tm),
        in_specs=[
            pl.BlockSpec((1, tm, k), lambda b, i: (b, i, 0)),
            pl.BlockSpec((k, d), lambda b, i: (0, 0)),
            pl.BlockSpec((1, tm, d), lambda b, i: (b, i, 0)),
            pl.BlockSpec((1, 1, d), lambda b, i: (b, 0, 0)),
        ],
        out_specs=pl.BlockSpec((1, tm, d), lambda b, i: (b, i, 0)),
        out_shape=jax.ShapeDtypeStruct((bsz, seq, d), F32),
        compiler_params=_params(("arbitrary", "arbitrary")),
        name="mm_res",
    )(a, w, x, gate)


def _ret_body(dl_ref, q_ref, k_ref, v_ref, gate_ref, o_ref, st_scr, cb_scr, *, nblk, tb, c):
    t = pl.program_id(2)
    nch = tb // c
    lg = jax.nn.log_sigmoid(dl_ref[0])
    lgf = lg[0:1, 0:1]
    lgb = lg[1:2, 0:1]
    idx = lax.broadcasted_iota(I32, (c, 1), 0).astype(F32)
    tn_dims = (((0,), (0,)), ((), ()))
    nt_dims = (((1,), (1,)), ((), ()))

    @pl.when((t == 0) | (t == nblk))
    def _():
        st_scr[...] = jnp.zeros_like(st_scr)

    @pl.when(t < nblk)
    def _():
        blk = nblk - 1 - t
        xi = jnp.exp(lgb * (c - idx))
        zeta = jnp.exp(lgb * idx)
        g_c = jnp.exp(lgb * c)
        for ci in reversed(range(nch)):
            rows = pl.ds(ci * c, c)
            q = q_ref[0, rows, :].astype(F32)
            k = k_ref[0, rows, :].astype(F32)
            v = v_ref[0, rows, :]
            st = st_scr[...]
            cross = jnp.dot((q * xi).astype(BF16), st.astype(BF16), preferred_element_type=F32)
            cb_scr[pl.ds(pl.multiple_of(blk * tb + ci * c, c), c), :] = cross.astype(BF16)
            st_scr[...] = st * g_c + lax.dot_general((k * zeta).astype(BF16), v, tn_dims,
                                                     preferred_element_type=F32)

    @pl.when(t >= nblk)
    def _():
        blk = t - nblk
        diff = (lax.broadcasted_iota(I32, (c, c), 0) - lax.broadcasted_iota(I32, (c, c), 1)).astype(F32)
        dmat = jnp.where(diff >= 0, jnp.exp(lgf * jnp.maximum(diff, 0.0)), jnp.exp(lgb * jnp.maximum(-diff, 0.0)))
        xi = jnp.exp(lgf * (idx + 1.0))
        zeta = jnp.exp(lgf * (c - 1.0 - idx))
        g_c = jnp.exp(lgf * c)
        for ci in range(nch):
            rows = pl.ds(ci * c, c)
            qb = q_ref[0, rows, :]
            kb = k_ref[0, rows, :]
            v = v_ref[0, rows, :]
            s = lax.dot_general(qb, kb, nt_dims, preferred_element_type=F32) * dmat
            o = jnp.dot(s.astype(BF16), v, preferred_element_type=F32)
            st = st_scr[...]
            o = o + jnp.dot((qb.astype(F32) * xi).astype(BF16), st.astype(BF16), preferred_element_type=F32)
            o = o + cb_scr[pl.ds(pl.multiple_of(blk * tb + ci * c, c), c), :].astype(F32)
            st_scr[...] = st * g_c + lax.dot_general((kb.astype(F32) * zeta).astype(BF16), v, tn_dims,
                                                     preferred_element_type=F32)
            o = o * lax.rsqrt(jnp.mean(o * o, axis=-1, keepdims=True) + EPS)
            o_ref[0, rows, :] = (_silu(gate_ref[0, rows, :].astype(F32)) * o).astype(o_ref.dtype)


def _retention_core(proj, decay_logit):
    bsz, seq, _ = proj.shape
    nh = RET_HEADS
    dk = proj.shape[2] // (6 * nh)
    dv = 2 * dk
    tb = min(RET_BLOCK, seq)
    c = min(RET_CHUNK, tb)
    nblk = seq // tb
    dl = jnp.broadcast_to(decay_logit.T[:, :, None].astype(F32), (nh, 2, LANES))

    def blk_of(t):
        return jnp.where(t < nblk, nblk - 1 - t, t - nblk)

    def fwd_blk(t):
        return jnp.where(t < nblk, 0, t - nblk)

    return pl.pallas_call(
        functools.partial(_ret_body, nblk=nblk, tb=tb, c=c),
        grid=(bsz, nh, 2 * nblk),
        in_specs=[
            pl.BlockSpec((1, 2, LANES), lambda b, h, t: (h, 0, 0)),
            pl.BlockSpec((1, tb, dk), lambda b, h, t: (b, blk_of(t), h)),
            pl.BlockSpec((1, tb, dk), lambda b, h, t: (b, blk_of(t), nh + h)),
            pl.BlockSpec((1, tb, dv), lambda b, h, t: (b, blk_of(t), nh + h)),
            pl.BlockSpec((1, tb, dv), lambda b, h, t: (b, fwd_blk(t), 2 * nh + h)),
        ],
        out_specs=pl.BlockSpec((1, tb, dv), lambda b, h, t: (b, fwd_blk(t), h)),
        out_shape=jax.ShapeDtypeStruct((bsz, seq, nh * dv), BF16),
        scratch_shapes=[pltpu.VMEM((dk, dv), F32), pltpu.VMEM((seq, dv), BF16)],
        compiler_params=_params(("arbitrary", "arbitrary", "arbitrary")),
        name="retention_core",
    )(dl, proj, proj, proj, proj)


def _gmlp_body(u_ref, v_ref, lng_ref, lnb_ref, ws_ref, bs_ref, wo_ref, x_ref, g_ref, o_ref, a_scr, *, tb, gd):
    v = v_ref[0].astype(F32)
    mu = jnp.mean(v, axis=-1, keepdims=True)
    vc = v - mu
    var = jnp.mean(vc * vc, axis=-1, keepdims=True)
    vn = ((vc * lax.rsqrt(var + EPS)) * lng_ref[...] + lnb_ref[...]).astype(BF16)
    for ci in range(tb // GM_CHUNK):
        r0 = ci * GM_CHUNK
        for gi in range(GM_GROUPS):
            c0 = gi * gd
            mixed = jnp.dot(ws_ref[gi], vn[r0:r0 + GM_CHUNK, c0:c0 + gd], preferred_element_type=F32) + bs_ref[gi]
            u = u_ref[0, r0:r0 + GM_CHUNK, c0:c0 + gd].astype(F32)
            a_scr[r0:r0 + GM_CHUNK, c0:c0 + gd] = (u * mixed).astype(BF16)
    y = jnp.dot(a_scr[...], wo_ref[...], preferred_element_type=F32)
    o_ref[0] = x_ref[0] + g_ref[0] * y


def _gmlp_core(z, ln_g, ln_b, w_s, b_s, w_out, x, gate, *, tb=256):
    bsz, seq, ffn = z.shape
    half = ffn // 2
    gd = half // GM_GROUPS
    d = w_out.shape[1]
    tb = min(tb, seq)
    return pl.pallas_call(
        functools.partial(_gmlp_body, tb=tb, gd=gd),
        grid=(bsz, seq // tb),
        in_specs=[
            pl.BlockSpec((1, tb, half), lambda b, i: (b, i, 0)),
            pl.BlockSpec((1, tb, half), lambda b, i: (b, i, 1)),
            pl.BlockSpec((1, half), lambda b, i: (0, 0)),
            pl.BlockSpec((1, half), lambda b, i: (0, 0)),
            pl.BlockSpec((GM_GROUPS, GM_CHUNK, GM_CHUNK), lambda b, i: (0, 0, 0)),
            pl.BlockSpec((GM_GROUPS, GM_CHUNK, 1), lambda b, i: (0, 0, 0)),
            pl.BlockSpec((half, d), lambda b, i: (0, 0)),
            pl.BlockSpec((1, tb, d), lambda b, i: (b, i, 0)),
            pl.BlockSpec((1, 1, d), lambda b, i: (b, 0, 0)),
        ],
        out_specs=pl.BlockSpec((1, tb, d), lambda b, i: (b, i, 0)),
        out_shape=jax.ShapeDtypeStruct((bsz, seq, d), F32),
        scratch_shapes=[pltpu.VMEM((tb, half), BF16)],
        compiler_params=_params(("arbitrary", "arbitrary")),
        name="gmlp_core",
    )(z, z, ln_g.reshape(1, half), ln_b.reshape(1, half), w_s.astype(BF16),
      b_s.reshape(GM_GROUPS, GM_CHUNK, 1), w_out, x, gate)


def _na_body(q_ref, kp_ref, kc_ref, kn_ref, vp_ref, vc_ref, vn_ref, bias_ref, o_ref, kw_scr, vw_scr, s_scr, p_scr,
             *, n_rows, rps, gw, hg, dh):
    i = pl.program_id(2)
    kw_scr[0:rps] = kp_ref[0]
    kw_scr[rps:2 * rps] = kc_ref[0]
    kw_scr[2 * rps:3 * rps] = kn_ref[0]
    vw_scr[0:rps] = vp_ref[0]
    vw_scr[rps:2 * rps] = vc_ref[0]
    vw_scr[2 * rps:3 * rps] = vn_ref[0]
    ch = hg * dh
    nkeys = NA_WIN_ROWS * gw
    head_of_ch = lax.broadcasted_iota(I32, (gw, ch), 1) // dh
    nt_dims = (((1,), (1,)), ((), ()))
    for rl in range(rps):
        r = i * rps + rl
        r0 = jnp.clip(r - NA_WIN_ROWS // 2, 0, n_rows - NA_WIN_ROWS)
        l0 = r0 - i * rps + rps
        ro0 = r0 - r + NA_WIN_ROWS - 1
        q = q_ref[0, rl]
        zero = jnp.zeros_like(q)
        qm = jnp.concatenate([jnp.where(head_of_ch == hh, q, zero) for hh in range(hg)], axis=0)
        kwin = kw_scr[pl.ds(l0, NA_WIN_ROWS)].reshape(nkeys, ch)
        vwin = vw_scr[pl.ds(l0, NA_WIN_ROWS)].reshape(nkeys, ch)
        s_scr[...] = lax.dot_general(qm, kwin, nt_dims, preferred_element_type=F32)
        inv = []
        for hh in range(hg):
            rows = slice(hh * gw, (hh + 1) * gw)
            sh = jnp.concatenate([s_scr[rows, jj * 2 * gw:(jj + 1) * 2 * gw] + bias_ref[ro0 + 2 * jj, 0, rows, :]
                                  for jj in range(NA_WIN_ROWS // 2)], axis=1)
            e = jnp.exp2(sh - jnp.max(sh, axis=-1, keepdims=True))
            inv.append(1.0 / jnp.sum(e, axis=-1, keepdims=True))
            p_scr[rows, :] = e.astype(BF16)
        pv = jnp.dot(p_scr[...], vwin, preferred_element_type=F32)
        out = jnp.zeros((gw, ch), F32)
        for hh in range(hg):
            out = out + jnp.where(head_of_ch == hh, pv[hh * gw:(hh + 1) * gw] * inv[hh], 0.0)
        o_ref[0, rl] = out.astype(o_ref.dtype)


def _na_bias_table(rpb, gw):
    nh = rpb.shape[0]
    kc = NA_WIN_COLS
    col = jnp.arange(gw)
    c0 = jnp.clip(col - kc // 2, 0, gw - kc)
    key = jnp.arange(gw)
    rel = key[None, :] - col[:, None] + kc - 1
    inside = (key[None, :] >= c0[:, None]) & (key[None, :] < c0[:, None] + kc)
    dense = jnp.where(inside[None, None], rpb[:, :, jnp.clip(rel, 0, 2 * kc - 2)].astype(F32) * LOG2_E,
                      NEG_BIAS)
    pair = jnp.concatenate([dense[:, :-1], dense[:, 1:]], axis=-1)
    pair = pair.reshape(nh // NA_HEAD_GROUP, NA_HEAD_GROUP, 2 * NA_WIN_ROWS - 2, gw, 2 * gw)
    return jnp.transpose(pair, (2, 0, 1, 3, 4)).reshape(2 * NA_WIN_ROWS - 2, nh // NA_HEAD_GROUP,
                                                       NA_HEAD_GROUP * gw, 2 * gw)


def _na_core(qkv, rpb):
    bsz, seq, d3 = qkv.shape
    d = d3 // 3
    gw = GRID_W
    n_rows = seq // gw
    rps = NA_ROWS_PER_STEP
    hg = NA_HEAD_GROUP
    dh = d // NA_HEADS
    ch = hg * dh
    ngrp = NA_HEADS // hg
    nb = n_rows // rps
    qkv4 = qkv.reshape(bsz, n_rows, gw, d3)
    bias = _na_bias_table(rpb, gw)
    n_pairs = bias.shape[0]

    def prev(i):
        return jnp.maximum(i - 1, 0)

    def nxt(i):
        return jnp.minimum(i + 1, nb - 1)

    def spec(row_fn, col_off):
        return pl.BlockSpec((1, rps, gw, ch), lambda b, g, i: (b, row_fn(i), 0, col_off + g))

    out = pl.pallas_call(
        functools.partial(_na_body, n_rows=n_rows, rps=rps, gw=gw, hg=hg, dh=dh),
        grid=(bsz, ngrp, nb),
        in_specs=[
            spec(lambda i: i, 0),
            spec(prev, ngrp), spec(lambda i: i, ngrp), spec(nxt, ngrp),
            spec(prev, 2 * ngrp), spec(lambda i: i, 2 * ngrp), spec(nxt, 2 * ngrp),
            pl.BlockSpec((n_pairs, 1, hg * gw, 2 * gw), lambda b, g, i: (0, g, 0, 0)),
        ],
        out_specs=pl.BlockSpec((1, rps, gw, ch), lambda b, g, i: (b, i, 0, g)),
        out_shape=jax.ShapeDtypeStruct((bsz, n_rows, gw, d), BF16),
        scratch_shapes=[pltpu.VMEM((3 * rps, gw, ch), BF16), pltpu.VMEM((3 * rps, gw, ch), BF16),
                        pltpu.VMEM((hg * gw, NA_WIN_ROWS * gw), F32), pltpu.VMEM((hg * gw, NA_WIN_ROWS * gw), BF16)],
        compiler_params=_params(("arbitrary", "arbitrary", "arbitrary")),
        name="na_core",
    )(qkv4, qkv4, qkv4, qkv4, qkv4, qkv4, qkv4, bias)
    return out.reshape(bsz, seq, d)


def _router_body(x_ref, g_ref, sc_ref, sh_ref, rwt_ref, rb_ref, hrow_ref, meta_ref, tot_ref, run_scr, *, tm):
    first = (pl.program_id(0) == 0) & (pl.program_id(1) == 0)

    @pl.when(first)
    def _():
        run_scr[...] = jnp.zeros_like(run_scr)

    h = _rms_modulate(x_ref[0], g_ref[...], sc_ref[0], sh_ref[0])
    nt_dims = (((1,), (1,)), ((), ()))
    logits = lax.dot_general(rwt_ref[...], h, nt_dims, preferred_element_type=F32,
                             precision=lax.Precision.HIGHEST)
    scores = jax.nn.sigmoid(logits)
    biased = scores + rb_ref[...]
    b = [biased[e:e + 1, :] for e in range(N_EXPERTS)]
    sc = [scores[e:e + 1, :] for e in range(N_EXPERTS)]

    best = None
    grp = jnp.zeros((1, tm), I32)
    for gi in range(N_GROUPS):
        gs = None
        for (pi, pj) in PAIRS:
            ps = b[4 * gi + pi] + b[4 * gi + pj]
            gs = ps if gs is None else jnp.maximum(gs, ps)
        if best is None:
            best = gs
        else:
            better = gs > best
            grp = jnp.where(better, gi, grp)
            best = jnp.where(better, gs, best)

    def pick(vals, kk):
        out = vals[kk]
        for gi in range(1, N_GROUPS):
            out = jnp.where(grp == gi, vals[4 * gi + kk], out)
        return out

    bsel = [pick(b, kk) for kk in range(EXPERTS_PER_GROUP)]
    ssel = [pick(sc, kk) for kk in range(EXPERTS_PER_GROUP)]
    chosen = []
    for kk in range(EXPERTS_PER_GROUP):
        beaten = jnp.zeros((1, tm), I32)
        for jj in range(EXPERTS_PER_GROUP):
            if jj == kk:
                continue
            wins = (bsel[jj] > bsel[kk]) | ((bsel[jj] == bsel[kk]) & (jj < kk))
            beaten = beaten + wins.astype(I32)
        chosen.append(beaten < 2)

    pid = jnp.zeros((1, tm), I32)
    s_lo = jnp.zeros((1, tm), F32)
    s_hi = jnp.zeros((1, tm), F32)
    for pidx, (pi, pj) in enumerate(PAIRS):
        hit = chosen[pi] & chosen[pj]
        pid = jnp.where(hit, pidx, pid)
        s_lo = jnp.where(hit, ssel[pi], s_lo)
        s_hi = jnp.where(hit, ssel[pj], s_hi)
    denom = s_lo + s_hi
    w_lo = s_lo / denom
    w_hi = s_hi / denom
    cls = grp * len(PAIRS) + pid

    onehot = (lax.broadcasted_iota(I32, (CLASS_ROWS, tm), 0) == cls).astype(F32)
    tri = (lax.broadcasted_iota(I32, (tm, tm), 0) < lax.broadcasted_iota(I32, (tm, tm), 1)).astype(BF16)
    before = jnp.dot(onehot.astype(BF16), tri, preferred_element_type=F32)
    run = run_scr[...]
    rank = jnp.sum(onehot * (before + run[:, 0:1]), axis=0, keepdims=True)
    run = run + jnp.sum(onehot, axis=1, keepdims=True)
    run_scr[...] = run
    tot_ref[...] = run

    meta = jnp.concatenate([cls.astype(F32), rank, w_lo, w_hi, w_lo, w_hi, jnp.zeros((2, tm), F32)], axis=0)
    meta_ref[...] = meta
    meta_t = jnp.concatenate([meta, jnp.zeros((LANES - 8, tm), F32)], axis=0).T

    lane = lax.broadcasted_iota(I32, (tm, LANES), 1)
    mbits = lax.bitcast_convert_type(meta_t, I32)
    payload = jnp.where((lane == 2) | (lane == 3), lax.shift_right_logical(mbits, 16),
                        jnp.where((lane == 4) | (lane == 5), mbits & 0xFFFF, 0))
    h0 = lax.bitcast_convert_type(h[:, 0:LANES].astype(BF16).astype(F32), I32)
    hrow_ref[pl.ds(0, tm, stride=ROW_TILES), :] = lax.bitcast_convert_type((h0 & HIGH16) | payload, F32)
    for j in range(1, ROW_TILES):
        hrow_ref[pl.ds(j, tm, stride=ROW_TILES), :] = h[:, j * LANES:(j + 1) * LANES]


def _router(x, g, sc, sh, router_w, router_b):
    bsz, seq, d = x.shape
    tm = min(ROUTER_TILE, seq)
    nt = seq // tm
    return pl.pallas_call(
        functools.partial(_router_body, tm=tm),
        grid=(bsz, nt),
        in_specs=[
            pl.BlockSpec((1, tm, d), lambda b, i: (b, i, 0)),
            pl.BlockSpec((1, d), lambda b, i: (0, 0)),
            pl.BlockSpec((1, 1, d), lambda b, i: (b, 0, 0)),
            pl.BlockSpec((1, 1, d), lambda b, i: (b, 0, 0)),
            pl.BlockSpec((N_EXPERTS, d), lambda b, i: (0, 0)),
            pl.BlockSpec((N_EXPERTS, 1), lambda b, i: (0, 0)),
        ],
        out_specs=[
            pl.BlockSpec((tm * ROW_TILES, LANES), lambda b, i: (b * nt + i, 0)),
            pl.BlockSpec((8, tm), lambda b, i: (0, b * nt + i)),
            pl.BlockSpec((CLASS_ROWS, LANES), lambda b, i: (0, 0)),
        ],
        out_shape=[
            jax.ShapeDtypeStruct((bsz * seq * ROW_TILES, LANES), F32),
            jax.ShapeDtypeStruct((8, bsz * seq), F32),
            jax.ShapeDtypeStruct((CLASS_ROWS, LANES), F32),
        ],
        scratch_shapes=[pltpu.VMEM((CLASS_ROWS, LANES), F32)],
        compiler_params=_params(("arbitrary", "arbitrary")),
        name="moe_router",
    )(x, g.reshape(1, d), sc, sh, router_w.T, router_b.reshape(N_EXPERTS, 1))


def _token_copy(src, src_tok, dst, dst_tok, sem):
    s = pl.multiple_of(src_tok * ROW_TILES, ROW_TILES)
    t = pl.multiple_of(dst_tok * ROW_TILES, ROW_TILES)
    return pltpu.make_async_copy(src.at[pl.ds(s, ROW_TILES)], dst.at[pl.ds(t, ROW_TILES)], sem)


def _tile_fill(zero_scr, dst, tile_idx, sem):
    rows = MOE_TILE * ROW_TILES
    return pltpu.make_async_copy(zero_scr, dst.at[pl.ds(pl.multiple_of(tile_idx * rows, rows), rows)], sem)


def _dispatch_body(slot_ref, fill_ref, h_ref, o_ref, zero_scr, sem, *, tb):
    @pl.when(pl.program_id(0) == 0)
    def _():
        zero_scr[...] = jnp.zeros_like(zero_scr)
        for c in range(2 * N_CLASSES):
            @pl.when(fill_ref[c] >= 0)
            def _():
                _tile_fill(zero_scr, o_ref, fill_ref[c], sem).start()
        for c in range(2 * N_CLASSES):
            @pl.when(fill_ref[c] >= 0)
            def _():
                _tile_fill(zero_scr, o_ref, fill_ref[c], sem).wait()

    def start(i, carry):
        _token_copy(h_ref, i, o_ref, slot_ref[i], sem).start()
        return carry

    def wait(i, carry):
        _token_copy(h_ref, i, o_ref, slot_ref[i], sem).wait()
        return carry

    lax.fori_loop(0, tb, start, 0, unroll=8)
    lax.fori_loop(0, tb, wait, 0, unroll=8)


def _dispatch(hrow, slot, fill_tiles, n_slots, *, tb=512):
    n_tok = hrow.shape[0] // ROW_TILES
    tb = min(tb, n_tok)
    return pl.pallas_call(
        functools.partial(_dispatch_body, tb=tb),
        grid=(n_tok // tb,),
        in_specs=[
            pl.BlockSpec((tb,), lambda i: (i,), memory_space=pltpu.SMEM),
            pl.BlockSpec(memory_space=pltpu.SMEM),
            pl.BlockSpec((tb * ROW_TILES, LANES), lambda i: (i, 0)),
        ],
        out_specs=pl.BlockSpec(memory_space=pl.ANY),
        out_shape=jax.ShapeDtypeStruct((n_slots * ROW_TILES, LANES), hrow.dtype),
        scratch_shapes=[pltpu.VMEM((MOE_TILE * ROW_TILES, LANES), hrow.dtype), pltpu.SemaphoreType.DMA(())],
        compiler_params=_params(("arbitrary",)),
        name="moe_dispatch",
    )(slot, fill_tiles, hrow)


def _expert_body(e0_ref, e1_ref, valid_ref, src_ref, hs_ref, wgu0_ref, wgu1_ref, wd0_ref, wd1_ref, o_ref,
                 *, tile, de):
    del e0_ref, e1_ref, src_ref
    i = pl.program_id(0)

    @pl.when(valid_ref[i] == 1)
    def _():
        w0 = lax.bitcast_convert_type(_token_cols(hs_ref, 0, tile), I32)
        pay = w0 & 0xFFFF
        w_lo = lax.bitcast_convert_type(lax.shift_left(pay[:, 2:3], 16) | pay[:, 4:5], F32)
        w_hi = lax.bitcast_convert_type(lax.shift_left(pay[:, 3:4], 16) | pay[:, 5:6], F32)
        cols = [lax.bitcast_convert_type(w0 & HIGH16, F32)]
        cols += [_token_cols(hs_ref, j, tile) for j in range(1, ROW_TILES)]
        h = jnp.concatenate(cols, axis=1).astype(BF16)
        gu0 = jnp.dot(h, wgu0_ref[0], preferred_element_type=F32)
        hid0 = (_silu(gu0[:, 0:de]) * gu0[:, de:2 * de]) * w_lo
        gu1 = jnp.dot(h, wgu1_ref[0], preferred_element_type=F32)
        hid1 = (_silu(gu1[:, 0:de]) * gu1[:, de:2 * de]) * w_hi
        y = jnp.dot(hid0.astype(BF16), wd0_ref[0], preferred_element_type=F32)
        y = y + jnp.dot(hid1.astype(BF16), wd1_ref[0], preferred_element_type=F32)
        _store_token_tiles(o_ref, y, tile)

    @pl.when(valid_ref[i] == 0)
    def _():
        o_ref[...] = jnp.zeros_like(o_ref)


def _experts(hs, tile_e0, tile_e1, tile_valid, tile_src, w_gate_up, w_down):
    d = ROW_TILES * LANES
    de = w_down.shape[1]
    n_tiles = hs.shape[0] // (MOE_TILE * ROW_TILES)
    blk = (MOE_TILE * ROW_TILES, LANES)
    grid_spec = pltpu.PrefetchScalarGridSpec(
        num_scalar_prefetch=4,
        grid=(n_tiles,),
        in_specs=[
            pl.BlockSpec(blk, lambda i, e0, e1, va, src: (src[i], 0)),
            pl.BlockSpec((1, d, 2 * de), lambda i, e0, e1, va, src: (e0[i], 0, 0)),
            pl.BlockSpec((1, d, 2 * de), lambda i, e0, e1, va, src: (e1[i], 0, 0)),
            pl.BlockSpec((1, de, d), lambda i, e0, e1, va, src: (e0[i], 0, 0)),
            pl.BlockSpec((1, de, d), lambda i, e0, e1, va, src: (e1[i], 0, 0)),
        ],
        out_specs=pl.BlockSpec(blk, lambda i, e0, e1, va, src: (i, 0)),
    )
    return pl.pallas_call(
        functools.partial(_expert_body, tile=MOE_TILE, de=de),
        grid_spec=grid_spec,
        out_shape=jax.ShapeDtypeStruct(hs.shape, F32),
        compiler_params=_params(("arbitrary",)),
        name="moe_experts",
    )(tile_e0, tile_e1, tile_valid, tile_src, hs, w_gate_up, w_gate_up, w_down, w_down)


def _combine_body(slot_ref, nslot_ref, ys_ref, x_ref, g_ref, o_ref, ybuf, sems, *, tb, n_steps):
    step = pl.program_id(0) * pl.num_programs(1) + pl.program_id(1)
    cur = step % 2

    def gather(slots, buf, do_start):
        def body(i, carry):
            cp = _token_copy(ys_ref, slots[i], ybuf.at[buf], i, sems.at[buf])
            if do_start:
                cp.start()
            else:
                cp.wait()
            return carry

        lax.fori_loop(0, tb, body, 0, unroll=8)

    @pl.when(step == 0)
    def _():
        gather(slot_ref, 0, True)

    @pl.when(step + 1 < n_steps)
    def _():
        gather(nslot_ref, 1 - cur, True)

    gather(slot_ref, cur, False)
    o_ref[0] = x_ref[0] + g_ref[0] * _load_token_tiles(ybuf.at[cur], tb)


def _combine(ys, slot, x, gate, *, tb=512):
    bsz, seq, d = x.shape
    tb = min(tb, seq)
    nt = seq // tb
    n_steps = bsz * nt
    return pl.pallas_call(
        functools.partial(_combine_body, tb=tb, n_steps=n_steps),
        grid=(bsz, nt),
        in_specs=[
            pl.BlockSpec((tb,), lambda b, i: (b * nt + i,), memory_space=pltpu.SMEM),
            pl.BlockSpec((tb,), lambda b, i: (jnp.minimum(b * nt + i + 1, n_steps - 1),), memory_space=pltpu.SMEM),
            pl.BlockSpec(memory_space=pl.ANY),
            pl.BlockSpec((1, tb, d), lambda b, i: (b, i, 0)),
            pl.BlockSpec((1, 1, d), lambda b, i: (b, 0, 0)),
        ],
        out_specs=pl.BlockSpec((1, tb, d), lambda b, i: (b, i, 0)),
        out_shape=jax.ShapeDtypeStruct((bsz, seq, d), F32),
        scratch_shapes=[pltpu.VMEM((2, tb * ROW_TILES, LANES), F32), pltpu.SemaphoreType.DMA((2,))],
        compiler_params=_params(("arbitrary", "arbitrary")),
        name="moe_combine",
    )(slot, slot, ys, x, gate)


_PAIR_LO = np.array([p[0] for p in PAIRS], np.int32)
_PAIR_HI = np.array([p[1] for p in PAIRS], np.int32)


def _moe(x, g, sc, sh, gate, router_w, router_b, w_gate_up, w_down):
    bsz, seq, d = x.shape
    n_tok = bsz * seq
    hrow, meta, tot = _router(x, g, sc, sh, router_w, router_b)

    counts = tot[:N_CLASSES, 0].astype(I32)
    tiles_per_class = (counts + MOE_TILE - 1) // MOE_TILE
    tile_end = jnp.cumsum(tiles_per_class)
    tile_start = tile_end - tiles_per_class
    cls = meta[0].astype(I32)
    rank = meta[1].astype(I32)
    cls_onehot = cls[:, None] == jnp.arange(N_CLASSES, dtype=I32)[None, :]
    slot = jnp.sum(jnp.where(cls_onehot, tile_start[None, :] * MOE_TILE, 0), axis=1) + rank

    n_tiles = n_tok // MOE_TILE + N_CLASSES
    n_slots = n_tiles * MOE_TILE
    tile_id = jnp.arange(n_tiles, dtype=I32)
    tile_valid = (tile_id < tile_end[-1]).astype(I32)
    last_valid = jnp.maximum(tile_end[-1] - 1, 0)
    tile_cls = jnp.sum((jnp.minimum(tile_id, last_valid)[:, None] >= tile_end[None, :]).astype(I32), axis=1)
    tile_cls = jnp.minimum(tile_cls, N_CLASSES - 1)
    grp = tile_cls // len(PAIRS)
    pair = tile_cls % len(PAIRS)
    tile_e0 = grp * EXPERTS_PER_GROUP + jnp.asarray(_PAIR_LO)[pair]
    tile_e1 = grp * EXPERTS_PER_GROUP + jnp.asarray(_PAIR_HI)[pair]

    tile_src = jnp.minimum(tile_id, last_valid)
    last_tile = jnp.where(tiles_per_class > 0, tile_end - 1, -1)
    unused = tile_end[-1] + jnp.arange(N_CLASSES, dtype=I32)
    fill_tiles = jnp.concatenate([last_tile, jnp.where(unused < n_tiles, unused, -1)])

    hs = _dispatch(hrow, slot, fill_tiles, n_slots)
    ys = _experts(hs, tile_e0, tile_e1, tile_valid, tile_src, w_gate_up, w_down)
    return _combine(ys, slot, x, gate)


def _rotary_tables(seq, half):
    inv_freq = RET_ROPE_BASE ** (-jnp.arange(half, dtype=F32) / half)
    ang = jnp.arange(seq, dtype=F32)[:, None] * inv_freq[None, :]
    return jnp.cos(ang), jnp.sin(ang)


def _retention_layer(x, g, sc, sh, gate, w_in, decay_logit, w_out):
    seq = x.shape[1]
    n = w_in.shape[1]
    nq = n // 6
    dk = nq // RET_HEADS
    cos, sin = _rotary_tables(seq, dk // 2)
    col_scale = jnp.concatenate([jnp.ones((nq,), F32), jnp.full((nq,), dk ** -0.5, F32), jnp.ones((n - 2 * nq,), F32)])
    proj = _norm_mm(x, g, sc, sh, w_in.astype(BF16), col_scale, epi="rot", cos=cos, sin=sin, n_rot_cols=2 * nq)
    a = _retention_core(proj, decay_logit)
    return _mm_res(a, w_out.astype(BF16), x, gate)


def _gmlp_layer(x, g, sc, sh, gate, w_in, ln_g, ln_b, w_s, b_s, w_out):
    n = w_in.shape[1]
    z = _norm_mm(x, g, sc, sh, w_in.astype(BF16), jnp.ones((n,), F32), epi="gelu")
    return _gmlp_core(z, ln_g, ln_b, w_s, b_s, w_out.astype(BF16), x, gate)


def _na_layer(x, g, sc, sh, gate, w_qkv, rpb, w_out):
    d = x.shape[2]
    dh = d // NA_HEADS
    col_scale = jnp.concatenate([jnp.full((d,), dh ** -0.5 * LOG2_E, F32), jnp.ones((2 * d,), F32)])
    qkv = _norm_mm(x, g, sc, sh, w_qkv.astype(BF16), col_scale)
    a = _na_core(qkv, rpb)
    return _mm_res(a, w_out.astype(BF16), x, gate)


def _final_norm_body(x_ref, g_ref, o_ref):
    x = x_ref[0]
    o_ref[0] = (x * lax.rsqrt(jnp.mean(x * x, axis=-1, keepdims=True) + EPS)) * g_ref[...]


def _final_norm(x, g, *, tm=1024):
    bsz, seq, d = x.shape
    tm = min(tm, seq)
    return pl.pallas_call(
        _final_norm_body,
        grid=(bsz, seq // tm),
        in_specs=[pl.BlockSpec((1, tm, d), lambda b, i: (b, i, 0)), pl.BlockSpec((1, d), lambda b, i: (0, 0))],
        out_specs=pl.BlockSpec((1, tm, d), lambda b, i: (b, i, 0)),
        out_shape=jax.ShapeDtypeStruct((bsz, seq, d), F32),
        compiler_params=_params(("arbitrary", "arbitrary")),
        name="final_norm",
    )(x, g.reshape(1, d))


def kernel(x, c, norm_g, final_g, w_ada, b_ada, ret_w_in, ret_decay_logit, ret_w_out, gm_w_in, gm_ln_g, gm_ln_b,
           gm_w_s, gm_b_s, gm_w_out, na_w_qkv, na_rpb, na_w_out, router_w, router_b, moe_w_gate_up, moe_w_down):
    depth = w_ada.shape[0]
    bsz, _, d = x.shape
    mod = _ada(c, w_ada, b_ada)
    for i in range(depth):
        kind, j = i % N_MIXERS, i // N_MIXERS
        sh1, sc1, g1, sh2, sc2, g2 = [mod[i, :, k * d:(k + 1) * d].reshape(bsz, 1, d) for k in range(6)]
        if kind == 0:
            x = _retention_layer(x, norm_g[i, 0], sc1, sh1, g1, ret_w_in[j], ret_decay_logit[j], ret_w_out[j])
        elif kind == 1:
            x = _gmlp_layer(x, norm_g[i, 0], sc1, sh1, g1, gm_w_in[j], gm_ln_g[j], gm_ln_b[j], gm_w_s[j],
                            gm_b_s[j], gm_w_out[j])
        else:
            x = _na_layer(x, norm_g[i, 0], sc1, sh1, g1, na_w_qkv[j], na_rpb[j], na_w_out[j])
        x = _moe(x, norm_g[i, 1], sc2, sh2, g2, router_w, router_b,
                 moe_w_gate_up[i].astype(BF16), moe_w_down[i].astype(BF16))
    return _final_norm(x, final_g)
```

```python
import functools

import jax
import jax.numpy as jnp
import numpy as np
from jax import lax
from jax.experimental import pallas as pl
from jax.experimental.pallas import tpu as pltpu

F32 = jnp.float32
BF16 = jnp.bfloat16
I32 = jnp.int32

EPS = 1e-6
N_MIXERS = 3
RET_HEADS = 4
RET_ROPE_BASE = 10000.0
GM_CHUNK = 128
GM_GROUPS = 4
GRID_W = 64
NA_HEADS = 32
NA_WIN_ROWS = 8
NA_WIN_COLS = 16
N_GROUPS = 4
EXPERTS_PER_GROUP = 4
N_EXPERTS = N_GROUPS * EXPERTS_PER_GROUP

LANES = 128
MXU_DIM = 256
VMEM_LIMIT_MIB = 56

RET_CHUNK = 256
RET_BLOCK = 1024
NA_ROWS_PER_STEP = 8
NA_HEAD_GROUP = 8
MOE_TILE = 256
ROUTER_TILE = 512
PAIRS = ((0, 1), (0, 2), (0, 3), (1, 3), (1, 2), (2, 3))
PAIR_SLOT_A = (0, 0, 0, 1, 1, 3)
PAIR_SLOT_B = (1, 2, 3, 3, 2, 2)
N_CLASSES = N_GROUPS * len(PAIRS)
CLASS_ROWS = 32
ROW_TILES = 8
HIGH16 = -65536
NEG_BIAS = -1e30
LOG2_E = float(np.log2(np.e))


def _params(semantics, vmem_mib=VMEM_LIMIT_MIB):
    return pltpu.CompilerParams(dimension_semantics=semantics, vmem_limit_bytes=vmem_mib * 2**20)


def _silu(x):
    return x * jax.nn.sigmoid(x)


def _gelu_exact(x):
    return 0.5 * x * (1.0 + lax.erf(x * np.float32(np.sqrt(0.5))))


def _token_cols(ref, j, n):
    return ref[pl.ds(j, n, stride=ROW_TILES), :]


def _store_token_tiles(dst, val, n):
    for j in range(ROW_TILES):
        dst[pl.ds(j, n, stride=ROW_TILES), :] = val[:, j * LANES:(j + 1) * LANES]


def _load_token_tiles(src, n):
    return jnp.concatenate([_token_cols(src, j, n) for j in range(ROW_TILES)], axis=1)


def _rms_modulate(x, g, sc, sh):
    y = x * lax.rsqrt(jnp.mean(x * x, axis=-1, keepdims=True) + EPS)
    return (y * g) * (1.0 + sc) + sh


def _ada_body(c_ref, w_ref, b_ref, o_ref):
    ca = _silu(c_ref[...]).astype(BF16)
    o_ref[0] = jnp.dot(ca, w_ref[0].astype(BF16), preferred_element_type=F32) + b_ref[0]


def _ada(c, w_ada, b_ada):
    depth, d, n = w_ada.shape
    bsz = c.shape[0]
    rows = 16
    tn = 1536
    c_pad = jnp.zeros((rows, d), F32).at[:bsz].set(c)
    out = pl.pallas_call(
        _ada_body,
        grid=(depth, n // tn),
        in_specs=[
            pl.BlockSpec((rows, d), lambda l, j: (0, 0)),
            pl.BlockSpec((1, d, tn), lambda l, j: (l, 0, j)),
            pl.BlockSpec((1, 1, tn), lambda l, j: (l, 0, j)),
        ],
        out_specs=pl.BlockSpec((1, rows, tn), lambda l, j: (l, 0, j)),
        out_shape=jax.ShapeDtypeStruct((depth, rows, n), F32),
        compiler_params=_params(("arbitrary", "arbitrary")),
        name="ada_mod",
    )(c_pad, w_ada, b_ada.reshape(depth, 1, n))
    return out[:, :bsz]


def _norm_mm_body(*refs, epi, tn, n_rot_tiles):
    if epi == "rot":
        x_ref, g_ref, sc_ref, sh_ref, w_ref, cs_ref, cos_ref, sin_ref, o_ref, h_scr = refs
    else:
        x_ref, g_ref, sc_ref, sh_ref, w_ref, cs_ref, o_ref, h_scr = refs
    j = pl.program_id(2)

    @pl.when(j == 0)
    def _():
        h_scr[...] = _rms_modulate(x_ref[0], g_ref[...], sc_ref[0], sh_ref[0]).astype(BF16)

    acc = jnp.dot(h_scr[...], w_ref[...], preferred_element_type=F32) * cs_ref[...]
    if epi == "gelu":
        o_ref[0] = _gelu_exact(acc).astype(o_ref.dtype)
    elif epi == "rot":
        @pl.when(j < n_rot_tiles)
        def _():
            cos = cos_ref[...]
            sin = sin_ref[...]
            for hh in range(tn // (2 * LANES)):
                lo = hh * 2 * LANES
                x1 = acc[:, lo:lo + LANES]
                x2 = acc[:, lo + LANES:lo + 2 * LANES]
                o_ref[0, :, lo:lo + LANES] = (x1 * cos - x2 * sin).astype(o_ref.dtype)
                o_ref[0, :, lo + LANES:lo + 2 * LANES] = (x1 * sin + x2 * cos).astype(o_ref.dtype)

        @pl.when(j >= n_rot_tiles)
        def _():
            o_ref[0] = acc.astype(o_ref.dtype)
    else:
        o_ref[0] = acc.astype(o_ref.dtype)


def _norm_mm(x, g, sc, sh, w, col_scale, *, epi="none", cos=None, sin=None, n_rot_cols=0, tm=1024, tn=1024):
    bsz, seq, d = x.shape
    n = w.shape[1]
    tm = min(tm, seq)
    in_specs = [
        pl.BlockSpec((1, tm, d), lambda b, i, j: (b, i, 0)),
        pl.BlockSpec((1, d), lambda b, i, j: (0, 0)),
        pl.BlockSpec((1, 1, d), lambda b, i, j: (b, 0, 0)),
        pl.BlockSpec((1, 1, d), lambda b, i, j: (b, 0, 0)),
        pl.BlockSpec((d, tn), lambda b, i, j: (0, j)),
        pl.BlockSpec((1, tn), lambda b, i, j: (0, j)),
    ]
    args = [x, g.reshape(1, d), sc, sh, w, col_scale.reshape(1, n)]
    if epi == "rot":
        in_specs += [pl.BlockSpec((tm, LANES), lambda b, i, j: (i, 0))] * 2
        args += [cos, sin]
    return pl.pallas_call(
        functools.partial(_norm_mm_body, epi=epi, tn=tn, n_rot_tiles=n_rot_cols // tn),
        grid=(bsz, seq // tm, n // tn),
        in_specs=in_specs,
        out_specs=pl.BlockSpec((1, tm, tn), lambda b, i, j: (b, i, j)),
        out_shape=jax.ShapeDtypeStruct((bsz, seq, n), BF16),
        scratch_shapes=[pltpu.VMEM((tm, d), BF16)],
        compiler_params=_params(("arbitrary", "arbitrary", "arbitrary")),
        name="norm_mm_" + epi,
    )(*args)


def _mm_res_body(a_ref, w_ref, x_ref, g_ref, o_ref):
    y = jnp.dot(a_ref[0], w_ref[...], preferred_element_type=F32)
    o_ref[0] = x_ref[0] + g_ref[0] * y


def _mm_res(a, w, x, gate, *, tm=512):
    bsz, seq, k = a.shape
    d = w.shape[1]
    tm = min(tm, seq)
    return pl.pallas_call(
        _mm_res_body,
        grid=(bsz, seq // tm),
        in_specs=[
            pl.BlockSpec((1, tm, k), lambda b, i: (b, i, 0)),
            pl.BlockSpec((k, d), lambda b, i: (0, 0)),
            pl.BlockSpec((1, tm, d), lambda b, i: (b, i, 0)),
            pl.BlockSpec((1, 1, d), lambda b, i: (b, 0, 0)),
        ],
        out_specs=pl.BlockSpec((1, tm, d), lambda b, i: (b, i, 0)),
        out_shape=jax.ShapeDtypeStruct((bsz, seq, d), F32),
        compiler_params=_params(("arbitrary", "arbitrary")),
        name="mm_res",
    )(a, w, x, gate)


def _ret_body(dl_ref, q_ref, k_ref, v_ref, gate_ref, o_ref, st_scr, cb_scr, *, nblk, tb, c, fused):
    t = pl.program_id(2)
    nch = tb // c
    lg = jax.nn.log_sigmoid(dl_ref[0])
    lgf = lg[0:1, 0:1]
    lgb = lg[1:2, 0:1]
    idx = lax.broadcasted_iota(I32, (c, 1), 0).astype(F32)
    tn_dims = (((0,), (0,)), ((), ()))
    nt_dims = (((1,), (1,)), ((), ()))

    @pl.when((t == 0) | (t == nblk))
    def _():
        st_scr[...] = jnp.zeros_like(st_scr)

    @pl.when(t < nblk)
    def _():
        blk = nblk - 1 - t
        xi = jnp.exp(lgb * (c - idx))
        zeta = jnp.exp(lgb * idx)
        g_c = jnp.exp(lgb * c)
        for ci in reversed(range(nch)):
            rows = pl.ds(ci * c, c)
            v = v_ref[0, rows, :]
            if fused:
                qs = q_ref[0, rows, :] * xi.astype(BF16)
                ks = k_ref[0, rows, :] * zeta.astype(BF16)
            else:
                qs = (q_ref[0, rows, :].astype(F32) * xi).astype(BF16)
                ks = (k_ref[0, rows, :].astype(F32) * zeta).astype(BF16)
            st = st_scr[...]
            cross = jnp.dot(qs, st.astype(BF16), preferred_element_type=F32)
            cb_scr[pl.ds(pl.multiple_of(blk * tb + ci * c, c), c), :] = cross.astype(BF16)
            st_scr[...] = st * g_c + lax.dot_general(ks, v, tn_dims, preferred_element_type=F32)

    @pl.when(t >= nblk)
    def _():
        blk = t - nblk
        diff = (lax.broadcasted_iota(I32, (c, c), 0) - lax.broadcasted_iota(I32, (c, c), 1)).astype(F32)
        dmat = jnp.where(diff >= 0, jnp.exp(lgf * jnp.maximum(diff, 0.0)), jnp.exp(lgb * jnp.maximum(-diff, 0.0)))
        xi = jnp.exp(lgf * (idx + 1.0))
        zeta = jnp.exp(lgf * (c - 1.0 - idx))
        g_c = jnp.exp(lgf * c)
        for ci in range(nch):
            rows = pl.ds(ci * c, c)
            qb = q_ref[0, rows, :]
            kb = k_ref[0, rows, :]
            v = v_ref[0, rows, :]
            s = lax.dot_general(qb, kb, nt_dims, preferred_element_type=F32) * dmat
            st = st_scr[...]
            if fused:
                qs = qb * xi.astype(BF16)
                ks = kb * zeta.astype(BF16)
                o = jnp.dot(jnp.concatenate([s.astype(BF16), qs], axis=1),
                            jnp.concatenate([v, st.astype(BF16)], axis=0), preferred_element_type=F32)
            else:
                qs = (qb.astype(F32) * xi).astype(BF16)
                ks = (kb.astype(F32) * zeta).astype(BF16)
                o = jnp.dot(s.astype(BF16), v, preferred_element_type=F32)
                o = o + jnp.dot(qs, st.astype(BF16), preferred_element_type=F32)
            o = o + cb_scr[pl.ds(pl.multiple_of(blk * tb + ci * c, c), c), :].astype(F32)
            st_scr[...] = st * g_c + lax.dot_general(ks, v, tn_dims, preferred_element_type=F32)
            o = o * lax.rsqrt(jnp.mean(o * o, axis=-1, keepdims=True) + EPS)
            o_ref[0, rows, :] = (_silu(gate_ref[0, rows, :].astype(F32)) * o).astype(o_ref.dtype)


def _retention_core(proj, decay_logit, fused):
    bsz, seq, _ = proj.shape
    nh = RET_HEADS
    dk = proj.shape[2] // (6 * nh)
    dv = 2 * dk
    tb = min(RET_BLOCK, seq)
    c = min(RET_CHUNK, tb)
    nblk = seq // tb
    dl = jnp.broadcast_to(decay_logit.T[:, :, None].astype(F32), (nh, 2, LANES))

    def blk_of(t):
        return jnp.where(t < nblk, nblk - 1 - t, t - nblk)

    def fwd_blk(t):
        return jnp.where(t < nblk, 0, t - nblk)

    return pl.pallas_call(
        functools.partial(_ret_body, nblk=nblk, tb=tb, c=c, fused=fused),
        grid=(bsz, nh, 2 * nblk),
        in_specs=[
            pl.BlockSpec((1, 2, LANES), lambda b, h, t: (h, 0, 0)),
            pl.BlockSpec((1, tb, dk), lambda b, h, t: (b, blk_of(t), h)),
            pl.BlockSpec((1, tb, dk), lambda b, h, t: (b, blk_of(t), nh + h)),
            pl.BlockSpec((1, tb, dv), lambda b, h, t: (b, blk_of(t), nh + h)),
            pl.BlockSpec((1, tb, dv), lambda b, h, t: (b, fwd_blk(t), 2 * nh + h)),
        ],
        out_specs=pl.BlockSpec((1, tb, dv), lambda b, h, t: (b, fwd_blk(t), h)),
        out_shape=jax.ShapeDtypeStruct((bsz, seq, nh * dv), BF16),
        scratch_shapes=[pltpu.VMEM((dk, dv), F32), pltpu.VMEM((seq, dv), BF16)],
        compiler_params=_params(("arbitrary", "arbitrary", "arbitrary")),
        name="retention_core",
    )(dl, proj, proj, proj, proj)


def _gmlp_body(u_ref, v_ref, lng_ref, lnb_ref, ws_ref, bs_ref, wo_ref, x_ref, g_ref, o_ref, a_scr, *, tb, gd):
    v = v_ref[0].astype(F32)
    mu = jnp.mean(v, axis=-1, keepdims=True)
    vc = v - mu
    var = jnp.mean(vc * vc, axis=-1, keepdims=True)
    vn = ((vc * lax.rsqrt(var + EPS)) * lng_ref[...] + lnb_ref[...]).astype(BF16)
    for ci in range(tb // GM_CHUNK):
        r0 = ci * GM_CHUNK
        for gi in range(GM_GROUPS):
            c0 = gi * gd
            mixed = jnp.dot(ws_ref[gi], vn[r0:r0 + GM_CHUNK, c0:c0 + gd], preferred_element_type=F32) + bs_ref[gi]
            u = u_ref[0, r0:r0 + GM_CHUNK, c0:c0 + gd].astype(F32)
            a_scr[r0:r0 + GM_CHUNK, c0:c0 + gd] = (u * mixed).astype(BF16)
    y = jnp.dot(a_scr[...], wo_ref[...], preferred_element_type=F32)
    o_ref[0] = x_ref[0] + g_ref[0] * y


def _gmlp_core(z, ln_g, ln_b, w_s, b_s, w_out, x, gate, *, tb=512):
    bsz, seq, ffn = z.shape
    half = ffn // 2
    gd = half // GM_GROUPS
    d = w_out.shape[1]
    tb = min(tb, seq)
    return pl.pallas_call(
        functools.partial(_gmlp_body, tb=tb, gd=gd),
        grid=(bsz, seq // tb),
        in_specs=[
            pl.BlockSpec((1, tb, half), lambda b, i: (b, i, 0)),
            pl.BlockSpec((1, tb, half), lambda b, i: (b, i, 1)),
            pl.BlockSpec((1, half), lambda b, i: (0, 0)),
            pl.BlockSpec((1, half), lambda b, i: (0, 0)),
            pl.BlockSpec((GM_GROUPS, GM_CHUNK, GM_CHUNK), lambda b, i: (0, 0, 0)),
            pl.BlockSpec((GM_GROUPS, GM_CHUNK, 1), lambda b, i: (0, 0, 0)),
            pl.BlockSpec((half, d), lambda b, i: (0, 0)),
            pl.BlockSpec((1, tb, d), lambda b, i: (b, i, 0)),
            pl.BlockSpec((1, 1, d), lambda b, i: (b, 0, 0)),
        ],
        out_specs=pl.BlockSpec((1, tb, d), lambda b, i: (b, i, 0)),
        out_shape=jax.ShapeDtypeStruct((bsz, seq, d), F32),
        scratch_shapes=[pltpu.VMEM((tb, half), BF16)],
        compiler_params=_params(("arbitrary", "arbitrary")),
        name="gmlp_core",
    )(z, z, ln_g.reshape(1, half), ln_b.reshape(1, half), w_s.astype(BF16),
      b_s.reshape(GM_GROUPS, GM_CHUNK, 1), w_out, x, gate)


def _na_body(q_ref, kp_ref, kc_ref, kn_ref, vp_ref, vc_ref, vn_ref, bias_ref, o_ref, kw_scr, vw_scr, s_scr, p_scr,
             *, n_rows, rps, gw, hg, dh):
    i = pl.program_id(2)
    kw_scr[0:rps] = kp_ref[0]
    kw_scr[rps:2 * rps] = kc_ref[0]
    kw_scr[2 * rps:3 * rps] = kn_ref[0]
    vw_scr[0:rps] = vp_ref[0]
    vw_scr[rps:2 * rps] = vc_ref[0]
    vw_scr[2 * rps:3 * rps] = vn_ref[0]
    ch = hg * dh
    nkeys = NA_WIN_ROWS * gw
    head_of_ch = lax.broadcasted_iota(I32, (gw, ch), 1) // dh
    nt_dims = (((1,), (1,)), ((), ()))
    for rl in range(rps):
        r = i * rps + rl
        r0 = jnp.clip(r - NA_WIN_ROWS // 2, 0, n_rows - NA_WIN_ROWS)
        l0 = r0 - i * rps + rps
        ro0 = r0 - r + NA_WIN_ROWS - 1
        q = q_ref[0, rl]
        zero = jnp.zeros_like(q)
        qm = jnp.concatenate([jnp.where(head_of_ch == hh, q, zero) for hh in range(hg)], axis=0)
        kwin = kw_scr[pl.ds(l0, NA_WIN_ROWS)].reshape(nkeys, ch)
        vwin = vw_scr[pl.ds(l0, NA_WIN_ROWS)].reshape(nkeys, ch)
        s_scr[...] = lax.dot_general(qm, kwin, nt_dims, preferred_element_type=F32)
        inv = []
        for hh in range(hg):
            rows = slice(hh * gw, (hh + 1) * gw)
            sh = jnp.concatenate([s_scr[rows, jj * 2 * gw:(jj + 1) * 2 * gw] + bias_ref[ro0 + 2 * jj, 0, rows, :]
                                  for jj in range(NA_WIN_ROWS // 2)], axis=1)
            e = jnp.exp2(sh - jnp.max(sh, axis=-1, keepdims=True))
            inv.append(1.0 / jnp.sum(e, axis=-1, keepdims=True))
            p_scr[rows, :] = e.astype(BF16)
        pv = jnp.dot(p_scr[...], vwin, preferred_element_type=F32)
        out = jnp.zeros((gw, ch), F32)
        for hh in range(hg):
            out = out + jnp.where(head_of_ch == hh, pv[hh * gw:(hh + 1) * gw] * inv[hh], 0.0)
        o_ref[0, rl] = out.astype(o_ref.dtype)


def _na_bias_table(rpb, gw):
    nh = rpb.shape[0]
    kc = NA_WIN_COLS
    col = jnp.arange(gw)
    c0 = jnp.clip(col - kc // 2, 0, gw - kc)
    key = jnp.arange(gw)
    rel = key[None, :] - col[:, None] + kc - 1
    inside = (key[None, :] >= c0[:, None]) & (key[None, :] < c0[:, None] + kc)
    dense = jnp.where(inside[None, None], rpb[:, :, jnp.clip(rel, 0, 2 * kc - 2)].astype(F32) * LOG2_E,
                      NEG_BIAS)
    pair = jnp.concatenate([dense[:, :-1], dense[:, 1:]], axis=-1)
    pair = pair.reshape(nh // NA_HEAD_GROUP, NA_HEAD_GROUP, 2 * NA_WIN_ROWS - 2, gw, 2 * gw)
    return jnp.transpose(pair, (2, 0, 1, 3, 4)).reshape(2 * NA_WIN_ROWS - 2, nh // NA_HEAD_GROUP,
                                                       NA_HEAD_GROUP * gw, 2 * gw)


def _na_core(qkv, rpb):
    bsz, seq, d3 = qkv.shape
    d = d3 // 3
    gw = GRID_W
    n_rows = seq // gw
    rps = NA_ROWS_PER_STEP
    hg = NA_HEAD_GROUP
    dh = d // NA_HEADS
    ch = hg * dh
    ngrp = NA_HEADS // hg
    nb = n_rows // rps
    qkv4 = qkv.reshape(bsz, n_rows, gw, d3)
    bias = _na_bias_table(rpb, gw)
    n_pairs = bias.shape[0]

    def prev(i):
        return jnp.maximum(i - 1, 0)

    def nxt(i):
        return jnp.minimum(i + 1, nb - 1)

    def spec(row_fn, col_off):
        return pl.BlockSpec((1, rps, gw, ch), lambda b, g, i: (b, row_fn(i), 0, col_off + g))

    out = pl.pallas_call(
        functools.partial(_na_body, n_rows=n_rows, rps=rps, gw=gw, hg=hg, dh=dh),
        grid=(bsz, ngrp, nb),
        in_specs=[
            spec(lambda i: i, 0),
            spec(prev, ngrp), spec(lambda i: i, ngrp), spec(nxt, ngrp),
            spec(prev, 2 * ngrp), spec(lambda i: i, 2 * ngrp), spec(nxt, 2 * ngrp),
            pl.BlockSpec((n_pairs, 1, hg * gw, 2 * gw), lambda b, g, i: (0, g, 0, 0)),
        ],
        out_specs=pl.BlockSpec((1, rps, gw, ch), lambda b, g, i: (b, i, 0, g)),
        out_shape=jax.ShapeDtypeStruct((bsz, n_rows, gw, d), BF16),
        scratch_shapes=[pltpu.VMEM((3 * rps, gw, ch), BF16), pltpu.VMEM((3 * rps, gw, ch), BF16),
                        pltpu.VMEM((hg * gw, NA_WIN_ROWS * gw), F32), pltpu.VMEM((hg * gw, NA_WIN_ROWS * gw), BF16)],
        compiler_params=_params(("arbitrary", "arbitrary", "arbitrary")),
        name="na_core",
    )(qkv4, qkv4, qkv4, qkv4, qkv4, qkv4, qkv4, bias)
    return out.reshape(bsz, seq, d)


def _router_body(x_ref, g_ref, sc_ref, sh_ref, rwt_ref, rb_ref, hrow_ref, meta_ref, tot_ref, run_scr, *, tm):
    first = (pl.program_id(0) == 0) & (pl.program_id(1) == 0)

    @pl.when(first)
    def _():
        run_scr[...] = jnp.zeros_like(run_scr)

    h = _rms_modulate(x_ref[0], g_ref[...], sc_ref[0], sh_ref[0])
    nt_dims = (((1,), (1,)), ((), ()))
    logits = lax.dot_general(rwt_ref[...], h, nt_dims, preferred_element_type=F32,
                             precision=lax.Precision.HIGHEST)
    scores = jax.nn.sigmoid(logits)
    biased = scores + rb_ref[...]
    b = [biased[e:e + 1, :] for e in range(N_EXPERTS)]
    sc = [scores[e:e + 1, :] for e in range(N_EXPERTS)]

    best = None
    grp = jnp.zeros((1, tm), I32)
    for gi in range(N_GROUPS):
        gs = None
        for (pi, pj) in PAIRS:
            ps = b[4 * gi + pi] + b[4 * gi + pj]
            gs = ps if gs is None else jnp.maximum(gs, ps)
        if best is None:
            best = gs
        else:
            better = gs > best
            grp = jnp.where(better, gi, grp)
            best = jnp.where(better, gs, best)

    def pick(vals, kk):
        out = vals[kk]
        for gi in range(1, N_GROUPS):
            out = jnp.where(grp == gi, vals[4 * gi + kk], out)
        return out

    bsel = [pick(b, kk) for kk in range(EXPERTS_PER_GROUP)]
    ssel = [pick(sc, kk) for kk in range(EXPERTS_PER_GROUP)]
    chosen = []
    for kk in range(EXPERTS_PER_GROUP):
        beaten = jnp.zeros((1, tm), I32)
        for jj in range(EXPERTS_PER_GROUP):
            if jj == kk:
                continue
            wins = (bsel[jj] > bsel[kk]) | ((bsel[jj] == bsel[kk]) & (jj < kk))
            beaten = beaten + wins.astype(I32)
        chosen.append(beaten < 2)

    pid = jnp.zeros((1, tm), I32)
    s_lo = jnp.zeros((1, tm), F32)
    s_hi = jnp.zeros((1, tm), F32)
    for pidx, (pi, pj) in enumerate(PAIRS):
        hit = chosen[pi] & chosen[pj]
        pid = jnp.where(hit, pidx, pid)
        s_lo = jnp.where(hit, ssel[pi], s_lo)
        s_hi = jnp.where(hit, ssel[pj], s_hi)
    denom = s_lo + s_hi
    w_lo = s_lo / denom
    w_hi = s_hi / denom
    cls = grp * len(PAIRS) + pid

    onehot = (lax.broadcasted_iota(I32, (CLASS_ROWS, tm), 0) == cls).astype(F32)
    tri = (lax.broadcasted_iota(I32, (tm, tm), 0) < lax.broadcasted_iota(I32, (tm, tm), 1)).astype(BF16)
    before = jnp.dot(onehot.astype(BF16), tri, preferred_element_type=F32)
    run = run_scr[...]
    rank = jnp.sum(onehot * (before + run[:, 0:1]), axis=0, keepdims=True)
    run = run + jnp.sum(onehot, axis=1, keepdims=True)
    run_scr[...] = run
    tot_ref[...] = run

    meta = jnp.concatenate([cls.astype(F32), rank, w_lo, w_hi, w_lo, w_hi, jnp.zeros((2, tm), F32)], axis=0)
    meta_ref[...] = meta
    meta_t = jnp.concatenate([meta, jnp.zeros((LANES - 8, tm), F32)], axis=0).T

    lane = lax.broadcasted_iota(I32, (tm, LANES), 1)
    mbits = lax.bitcast_convert_type(meta_t, I32)
    payload = jnp.where((lane == 2) | (lane == 3), lax.shift_right_logical(mbits, 16),
                        jnp.where((lane == 4) | (lane == 5), mbits & 0xFFFF, 0))
    h0 = lax.bitcast_convert_type(h[:, 0:LANES].astype(BF16).astype(F32), I32)
    hrow_ref[pl.ds(0, tm, stride=ROW_TILES), :] = lax.bitcast_convert_type((h0 & HIGH16) | payload, F32)
    for j in range(1, ROW_TILES):
        hrow_ref[pl.ds(j, tm, stride=ROW_TILES), :] = h[:, j * LANES:(j + 1) * LANES]


def _router(x, g, sc, sh, router_w, router_b):
    bsz, seq, d = x.shape
    tm = min(ROUTER_TILE, seq)
    nt = seq // tm
    return pl.pallas_call(
        functools.partial(_router_body, tm=tm),
        grid=(bsz, nt),
        in_specs=[
            pl.BlockSpec((1, tm, d), lambda b, i: (b, i, 0)),
            pl.BlockSpec((1, d), lambda b, i: (0, 0)),
            pl.BlockSpec((1, 1, d), lambda b, i: (b, 0, 0)),
            pl.BlockSpec((1, 1, d), lambda b, i: (b, 0, 0)),
            pl.BlockSpec((N_EXPERTS, d), lambda b, i: (0, 0)),
            pl.BlockSpec((N_EXPERTS, 1), lambda b, i: (0, 0)),
        ],
        out_specs=[
            pl.BlockSpec((tm * ROW_TILES, LANES), lambda b, i: (b * nt + i, 0)),
            pl.BlockSpec((8, tm), lambda b, i: (0, b * nt + i)),
            pl.BlockSpec((CLASS_ROWS, LANES), lambda b, i: (0, 0)),
        ],
        out_shape=[
            jax.ShapeDtypeStruct((bsz * seq * ROW_TILES, LANES), F32),
            jax.ShapeDtypeStruct((8, bsz * seq), F32),
            jax.ShapeDtypeStruct((CLASS_ROWS, LANES), F32),
        ],
        scratch_shapes=[pltpu.VMEM((CLASS_ROWS, LANES), F32)],
        compiler_params=_params(("arbitrary", "arbitrary")),
        name="moe_router",
    )(x, g.reshape(1, d), sc, sh, router_w.T, router_b.reshape(N_EXPERTS, 1))


def _token_copy(src, src_tok, dst, dst_tok, sem):
    s = pl.multiple_of(src_tok * ROW_TILES, ROW_TILES)
    t = pl.multiple_of(dst_tok * ROW_TILES, ROW_TILES)
    return pltpu.make_async_copy(src.at[pl.ds(s, ROW_TILES)], dst.at[pl.ds(t, ROW_TILES)], sem)


def _tile_fill(zero_scr, dst, tile_idx, sem):
    rows = MOE_TILE * ROW_TILES
    return pltpu.make_async_copy(zero_scr, dst.at[pl.ds(pl.multiple_of(tile_idx * rows, rows), rows)], sem)


def _dispatch_body(slot_ref, fill_ref, h_ref, o_ref, zero_scr, sem, *, tb):
    @pl.when(pl.program_id(0) == 0)
    def _():
        zero_scr[...] = jnp.zeros_like(zero_scr)
        for c in range(2 * N_CLASSES):
            @pl.when(fill_ref[c] >= 0)
            def _():
                _tile_fill(zero_scr, o_ref, fill_ref[c], sem).start()
        for c in range(2 * N_CLASSES):
            @pl.when(fill_ref[c] >= 0)
            def _():
                _tile_fill(zero_scr, o_ref, fill_ref[c], sem).wait()

    def start(i, carry):
        _token_copy(h_ref, i, o_ref, slot_ref[i], sem).start()
        return carry

    def wait(i, carry):
        _token_copy(h_ref, i, o_ref, slot_ref[i], sem).wait()
        return carry

    lax.fori_loop(0, tb, start, 0, unroll=8)
    lax.fori_loop(0, tb, wait, 0, unroll=8)


def _dispatch(hrow, slot, fill_tiles, n_slots, *, tb=512):
    n_tok = hrow.shape[0] // ROW_TILES
    tb = min(tb, n_tok)
    return pl.pallas_call(
        functools.partial(_dispatch_body, tb=tb),
        grid=(n_tok // tb,),
        in_specs=[
            pl.BlockSpec((tb,), lambda i: (i,), memory_space=pltpu.SMEM),
            pl.BlockSpec(memory_space=pltpu.SMEM),
            pl.BlockSpec((tb * ROW_TILES, LANES), lambda i: (i, 0)),
        ],
        out_specs=pl.BlockSpec(memory_space=pl.ANY),
        out_shape=jax.ShapeDtypeStruct((n_slots * ROW_TILES, LANES), hrow.dtype),
        scratch_shapes=[pltpu.VMEM((MOE_TILE * ROW_TILES, LANES), hrow.dtype), pltpu.SemaphoreType.DMA(())],
        compiler_params=_params(("arbitrary",)),
        name="moe_dispatch",
    )(slot, fill_tiles, hrow)


def _expert_body(ea_ref, eb_ref, flags_ref, src_ref, hs_ref, wgua_ref, wgub_ref, wda_ref, wdb_ref, o_ref,
                 wgua_scr, wgub_scr, wda_scr, wdb_scr, *, tile, de):
    del ea_ref, eb_ref, src_ref
    flags = flags_ref[pl.program_id(0)]

    @pl.when((flags & 2) != 0)
    def _():
        wgua_scr[...] = wgua_ref[0].astype(BF16)
        wda_scr[...] = wda_ref[0].astype(BF16)

    @pl.when((flags & 4) != 0)
    def _():
        wgub_scr[...] = wgub_ref[0].astype(BF16)
        wdb_scr[...] = wdb_ref[0].astype(BF16)

    @pl.when((flags & 1) != 0)
    def _():
        w0 = lax.bitcast_convert_type(_token_cols(hs_ref, 0, tile), I32)
        pay = w0 & 0xFFFF
        w_lo = lax.bitcast_convert_type(lax.shift_left(pay[:, 2:3], 16) | pay[:, 4:5], F32)
        w_hi = lax.bitcast_convert_type(lax.shift_left(pay[:, 3:4], 16) | pay[:, 5:6], F32)
        swapped = (flags & 8) != 0
        w_a = jnp.where(swapped, w_hi, w_lo)
        w_b = jnp.where(swapped, w_lo, w_hi)
        cols = [lax.bitcast_convert_type(w0 & HIGH16, F32)]
        cols += [_token_cols(hs_ref, j, tile) for j in range(1, ROW_TILES)]
        h = jnp.concatenate(cols, axis=1).astype(BF16)
        gua = jnp.dot(h, wgua_scr[...], preferred_element_type=F32)
        hida = (_silu(gua[:, 0:de]) * gua[:, de:2 * de]) * w_a
        gub = jnp.dot(h, wgub_scr[...], preferred_element_type=F32)
        hidb = (_silu(gub[:, 0:de]) * gub[:, de:2 * de]) * w_b
        y = jnp.dot(hida.astype(BF16), wda_scr[...], preferred_element_type=F32)
        y = y + jnp.dot(hidb.astype(BF16), wdb_scr[...], preferred_element_type=F32)
        _store_token_tiles(o_ref, y, tile)

    @pl.when((flags & 1) == 0)
    def _():
        o_ref[...] = jnp.zeros_like(o_ref)


def _experts(hs, tile_ea, tile_eb, tile_flags, tile_src, w_gate_up, w_down):
    d = ROW_TILES * LANES
    de = w_down.shape[1]
    n_tiles = hs.shape[0] // (MOE_TILE * ROW_TILES)
    blk = (MOE_TILE * ROW_TILES, LANES)
    grid_spec = pltpu.PrefetchScalarGridSpec(
        num_scalar_prefetch=4,
        grid=(n_tiles,),
        in_specs=[
            pl.BlockSpec(blk, lambda i, ea, eb, fl, src: (src[i], 0)),
            pl.BlockSpec((1, d, 2 * de), lambda i, ea, eb, fl, src: (ea[i], 0, 0)),
            pl.BlockSpec((1, d, 2 * de), lambda i, ea, eb, fl, src: (eb[i], 0, 0)),
            pl.BlockSpec((1, de, d), lambda i, ea, eb, fl, src: (ea[i], 0, 0)),
            pl.BlockSpec((1, de, d), lambda i, ea, eb, fl, src: (eb[i], 0, 0)),
        ],
        out_specs=pl.BlockSpec(blk, lambda i, ea, eb, fl, src: (i, 0)),
        scratch_shapes=[pltpu.VMEM((d, 2 * de), BF16), pltpu.VMEM((d, 2 * de), BF16),
                        pltpu.VMEM((de, d), BF16), pltpu.VMEM((de, d), BF16)],
    )
    return pl.pallas_call(
        functools.partial(_expert_body, tile=MOE_TILE, de=de),
        grid_spec=grid_spec,
        out_shape=jax.ShapeDtypeStruct(hs.shape, F32),
        compiler_params=_params(("arbitrary",)),
        name="moe_experts",
    )(tile_ea, tile_eb, tile_flags, tile_src, hs, w_gate_up, w_gate_up, w_down, w_down)


def _combine_body(slot_ref, nslot_ref, ys_ref, x_ref, g_ref, o_ref, ybuf, sems, *, tb, n_steps):
    step = pl.program_id(0) * pl.num_programs(1) + pl.program_id(1)
    cur = step % 2

    def gather(slots, buf, do_start):
        def body(i, carry):
            cp = _token_copy(ys_ref, slots[i], ybuf.at[buf], i, sems.at[buf])
            if do_start:
                cp.start()
            else:
                cp.wait()
            return carry

        lax.fori_loop(0, tb, body, 0, unroll=8)

    @pl.when(step == 0)
    def _():
        gather(slot_ref, 0, True)

    @pl.when(step + 1 < n_steps)
    def _():
        gather(nslot_ref, 1 - cur, True)

    gather(slot_ref, cur, False)
    o_ref[0] = x_ref[0] + g_ref[0] * _load_token_tiles(ybuf.at[cur], tb)


def _combine(ys, slot, x, gate, *, tb=512):
    bsz, seq, d = x.shape
    tb = min(tb, seq)
    nt = seq // tb
    n_steps = bsz * nt
    return pl.pallas_call(
        functools.partial(_combine_body, tb=tb, n_steps=n_steps),
        grid=(bsz, nt),
        in_specs=[
            pl.BlockSpec((tb,), lambda b, i: (b * nt + i,), memory_space=pltpu.SMEM),
            pl.BlockSpec((tb,), lambda b, i: (jnp.minimum(b * nt + i + 1, n_steps - 1),), memory_space=pltpu.SMEM),
            pl.BlockSpec(memory_space=pl.ANY),
            pl.BlockSpec((1, tb, d), lambda b, i: (b, i, 0)),
            pl.BlockSpec((1, 1, d), lambda b, i: (b, 0, 0)),
        ],
        out_specs=pl.BlockSpec((1, tb, d), lambda b, i: (b, i, 0)),
        out_shape=jax.ShapeDtypeStruct((bsz, seq, d), F32),
        scratch_shapes=[pltpu.VMEM((2, tb * ROW_TILES, LANES), F32), pltpu.SemaphoreType.DMA((2,))],
        compiler_params=_params(("arbitrary", "arbitrary")),
        name="moe_combine",
    )(slot, slot, ys, x, gate)


def _moe(x, g, sc, sh, gate, router_w, router_b, w_gate_up, w_down, layer):
    bsz, seq, d = x.shape
    n_tok = bsz * seq
    hrow, meta, tot = _router(x, g, sc, sh, router_w, router_b)

    counts = tot[:N_CLASSES, 0].astype(I32)
    tiles_per_class = (counts + MOE_TILE - 1) // MOE_TILE
    tile_end = jnp.cumsum(tiles_per_class)
    tile_start = tile_end - tiles_per_class
    cls = meta[0].astype(I32)
    rank = meta[1].astype(I32)
    cls_onehot = cls[:, None] == jnp.arange(N_CLASSES, dtype=I32)[None, :]
    slot = jnp.sum(jnp.where(cls_onehot, tile_start[None, :] * MOE_TILE, 0), axis=1) + rank

    n_tiles = n_tok // MOE_TILE + N_CLASSES
    n_slots = n_tiles * MOE_TILE
    tile_id = jnp.arange(n_tiles, dtype=I32)
    tile_valid = (tile_id < tile_end[-1]).astype(I32)
    last_valid = jnp.maximum(tile_end[-1] - 1, 0)
    tile_src = jnp.minimum(tile_id, last_valid)
    tile_cls = jnp.sum((tile_src[:, None] >= tile_end[None, :]).astype(I32), axis=1)
    tile_cls = jnp.minimum(tile_cls, N_CLASSES - 1)
    grp = tile_cls // len(PAIRS)
    pair = tile_cls % len(PAIRS)
    base = layer * N_EXPERTS + grp * EXPERTS_PER_GROUP
    tile_ea = base + jnp.asarray(np.array(PAIR_SLOT_A, np.int32))[pair]
    tile_eb = base + jnp.asarray(np.array(PAIR_SLOT_B, np.int32))[pair]
    swapped = jnp.asarray(np.array([a > b for a, b in zip(PAIR_SLOT_A, PAIR_SLOT_B)], np.int32))[pair]

    def changed(e):
        return jnp.concatenate([jnp.ones((1,), I32), (e[1:] != e[:-1]).astype(I32)])

    tile_flags = tile_valid + 2 * changed(tile_ea) + 4 * changed(tile_eb) + 8 * swapped
    last_tile = jnp.where(tiles_per_class > 0, tile_end - 1, -1)
    unused = tile_end[-1] + jnp.arange(N_CLASSES, dtype=I32)
    fill_tiles = jnp.concatenate([last_tile, jnp.where(unused < n_tiles, unused, -1)])

    hs = _dispatch(hrow, slot, fill_tiles, n_slots)
    ys = _experts(hs, tile_ea, tile_eb, tile_flags, tile_src, w_gate_up, w_down)
    return _combine(ys, slot, x, gate)


def _rotary_tables(seq, half):
    inv_freq = RET_ROPE_BASE ** (-jnp.arange(half, dtype=F32) / half)
    ang = jnp.arange(seq, dtype=F32)[:, None] * inv_freq[None, :]
    return jnp.cos(ang), jnp.sin(ang)


def _retention_layer(x, g, sc, sh, gate, w_in, decay_logit, w_out, variant):
    seq = x.shape[1]
    n = w_in.shape[1]
    nq = n // 6
    dk = nq // RET_HEADS
    cos, sin = _rotary_tables(seq, dk // 2)
    col_scale = jnp.concatenate([jnp.ones((nq,), F32), jnp.full((nq,), dk ** -0.5, F32), jnp.ones((n - 2 * nq,), F32)])
    tiles = dict(tm=2048, tn=512) if variant else dict(tm=1024, tn=1024)
    proj = _norm_mm(x, g, sc, sh, w_in.astype(BF16), col_scale, epi="rot", cos=cos, sin=sin, n_rot_cols=2 * nq,
                    **tiles)
    a = _retention_core(proj, decay_logit, fused=bool(variant))
    return _mm_res(a, w_out.astype(BF16), x, gate)


def _gmlp_layer(x, g, sc, sh, gate, w_in, ln_g, ln_b, w_s, b_s, w_out):
    n = w_in.shape[1]
    z = _norm_mm(x, g, sc, sh, w_in.astype(BF16), jnp.ones((n,), F32), epi="gelu")
    return _gmlp_core(z, ln_g, ln_b, w_s, b_s, w_out.astype(BF16), x, gate)


def _na_layer(x, g, sc, sh, gate, w_qkv, rpb, w_out):
    d = x.shape[2]
    dh = d // NA_HEADS
    col_scale = jnp.concatenate([jnp.full((d,), dh ** -0.5 * LOG2_E, F32), jnp.ones((2 * d,), F32)])
    qkv = _norm_mm(x, g, sc, sh, w_qkv.astype(BF16), col_scale)
    a = _na_core(qkv, rpb)
    return _mm_res(a, w_out.astype(BF16), x, gate)


def _final_norm_body(x_ref, g_ref, o_ref):
    x = x_ref[0]
    o_ref[0] = (x * lax.rsqrt(jnp.mean(x * x, axis=-1, keepdims=True) + EPS)) * g_ref[...]


def _final_norm(x, g, *, tm=1024):
    bsz, seq, d = x.shape
    tm = min(tm, seq)
    return pl.pallas_call(
        _final_norm_body,
        grid=(bsz, seq // tm),
        in_specs=[pl.BlockSpec((1, tm, d), lambda b, i: (b, i, 0)), pl.BlockSpec((1, d), lambda b, i: (0, 0))],
        out_specs=pl.BlockSpec((1, tm, d), lambda b, i: (b, i, 0)),
        out_shape=jax.ShapeDtypeStruct((bsz, seq, d), F32),
        compiler_params=_params(("arbitrary", "arbitrary")),
        name="final_norm",
    )(x, g.reshape(1, d))


def kernel(x, c, norm_g, final_g, w_ada, b_ada, ret_w_in, ret_decay_logit, ret_w_out, gm_w_in, gm_ln_g, gm_ln_b,
           gm_w_s, gm_b_s, gm_w_out, na_w_qkv, na_rpb, na_w_out, router_w, router_b, moe_w_gate_up, moe_w_down):
    depth = w_ada.shape[0]
    bsz, _, d = x.shape
    mod = _ada(c, w_ada, b_ada)
    w_gate_up_all = moe_w_gate_up.reshape((-1,) + moe_w_gate_up.shape[2:])
    w_down_all = moe_w_down.reshape((-1,) + moe_w_down.shape[2:])
    for i in range(depth):
        kind, j = i % N_MIXERS, i // N_MIXERS
        sh1, sc1, g1, sh2, sc2, g2 = [mod[i, :, k * d:(k + 1) * d].reshape(bsz, 1, d) for k in range(6)]
        if kind == 0:
            x = _retention_layer(x, norm_g[i, 0], sc1, sh1, g1, ret_w_in[j], ret_decay_logit[j], ret_w_out[j],
                                 variant=(j == 0))
        elif kind == 1:
            x = _gmlp_layer(x, norm_g[i, 0], sc1, sh1, g1, gm_w_in[j], gm_ln_g[j], gm_ln_b[j], gm_w_s[j],
                            gm_b_s[j], gm_w_out[j])
        else:
            x = _na_layer(x, norm_g[i, 0], sc1, sh1, g1, na_w_qkv[j], na_rpb[j], na_w_out[j])
        x = _moe(x, norm_g[i, 1], sc2, sh2, g2, router_w, router_b, w_gate_up_all, w_down_all, i)
    return _final_norm(x, final_g)
```

```python
import functools

import jax
import jax.numpy as jnp
import numpy as np
from jax import lax
from jax.experimental import pallas as pl
from jax.experimental.pallas import tpu as pltpu

F32 = jnp.float32
BF16 = jnp.bfloat16
I32 = jnp.int32

EPS = 1e-6
N_MIXERS = 3
RET_HEADS = 4
RET_ROPE_BASE = 10000.0
GM_CHUNK = 128
GM_GROUPS = 4
GRID_W = 64
NA_HEADS = 32
NA_WIN_ROWS = 8
NA_WIN_COLS = 16
N_GROUPS = 4
EXPERTS_PER_GROUP = 4
N_EXPERTS = N_GROUPS * EXPERTS_PER_GROUP

LANES = 128
MXU_DIM = 256
VMEM_LIMIT_MIB = 56

RET_CHUNK = 256
RET_BLOCK = 1024
NA_ROWS_PER_STEP = 8
NA_HEAD_GROUP = 8
MOE_TILE = 256
ROUTER_TILE = 512
PAIRS = ((0, 1), (0, 2), (0, 3), (1, 3), (1, 2), (2, 3))
PAIR_SLOT_A = (0, 0, 0, 1, 1, 3)
PAIR_SLOT_B = (1, 2, 3, 3, 2, 2)
N_CLASSES = N_GROUPS * len(PAIRS)
CLASS_ROWS = 32
ROW_TILES = 8
HIGH16 = -65536
NEG_BIAS = -1e30
LOG2_E = float(np.log2(np.e))


def _params(semantics, vmem_mib=VMEM_LIMIT_MIB):
    return pltpu.CompilerParams(dimension_semantics=semantics, vmem_limit_bytes=vmem_mib * 2**20)


def _silu(x):
    return x * jax.nn.sigmoid(x)


def _gelu_exact(x):
    return 0.5 * x * (1.0 + lax.erf(x * np.float32(np.sqrt(0.5))))


def _token_cols(ref, j, n):
    return ref[pl.ds(j, n, stride=ROW_TILES), :]


def _store_token_tiles(dst, val, n):
    for j in range(ROW_TILES):
        dst[pl.ds(j, n, stride=ROW_TILES), :] = val[:, j * LANES:(j + 1) * LANES]


def _load_token_tiles(src, n):
    return jnp.concatenate([_token_cols(src, j, n) for j in range(ROW_TILES)], axis=1)


def _rms_modulate(x, g, sc, sh):
    y = x * lax.rsqrt(jnp.mean(x * x, axis=-1, keepdims=True) + EPS)
    return (y * g) * (1.0 + sc) + sh


def _ada_body(c_ref, w_ref, b_ref, o_ref):
    ca = _silu(c_ref[...]).astype(BF16)
    o_ref[0] = jnp.dot(ca, w_ref[0].astype(BF16), preferred_element_type=F32) + b_ref[0]


def _ada(c, w_ada, b_ada):
    depth, d, n = w_ada.shape
    bsz = c.shape[0]
    rows = 16
    tn = 1536
    c_pad = jnp.zeros((rows, d), F32).at[:bsz].set(c)
    out = pl.pallas_call(
        _ada_body,
        grid=(depth, n // tn),
        in_specs=[
            pl.BlockSpec((rows, d), lambda l, j: (0, 0)),
            pl.BlockSpec((1, d, tn), lambda l, j: (l, 0, j)),
            pl.BlockSpec((1, 1, tn), lambda l, j: (l, 0, j)),
        ],
        out_specs=pl.BlockSpec((1, rows, tn), lambda l, j: (l, 0, j)),
        out_shape=jax.ShapeDtypeStruct((depth, rows, n), F32),
        compiler_params=_params(("arbitrary", "arbitrary")),
        name="ada_mod",
    )(c_pad, w_ada, b_ada.reshape(depth, 1, n))
    return out[:, :bsz]


def _norm_mm_body(*refs, epi, tn, n_rot_tiles):
    if epi == "rot":
        x_ref, g_ref, sc_ref, sh_ref, w_ref, cs_ref, cos_ref, sin_ref, o_ref, h_scr = refs
    else:
        x_ref, g_ref, sc_ref, sh_ref, w_ref, cs_ref, o_ref, h_scr = refs
    j = pl.program_id(2)

    @pl.when(j == 0)
    def _():
        h_scr[...] = _rms_modulate(x_ref[0], g_ref[...], sc_ref[0], sh_ref[0]).astype(BF16)

    acc = jnp.dot(h_scr[...], w_ref[...], preferred_element_type=F32) * cs_ref[...]
    if epi == "gelu":
        o_ref[0] = _gelu_exact(acc).astype(o_ref.dtype)
    elif epi == "rot":
        @pl.when(j < n_rot_tiles)
        def _():
            cos = cos_ref[...]
            sin = sin_ref[...]
            for hh in range(tn // (2 * LANES)):
                lo = hh * 2 * LANES
                x1 = acc[:, lo:lo + LANES]
                x2 = acc[:, lo + LANES:lo + 2 * LANES]
                o_ref[0, :, lo:lo + LANES] = (x1 * cos - x2 * sin).astype(o_ref.dtype)
                o_ref[0, :, lo + LANES:lo + 2 * LANES] = (x1 * sin + x2 * cos).astype(o_ref.dtype)

        @pl.when(j >= n_rot_tiles)
        def _():
            o_ref[0] = acc.astype(o_ref.dtype)
    else:
        o_ref[0] = acc.astype(o_ref.dtype)


def _norm_mm(x, g, sc, sh, w, col_scale, *, epi="none", cos=None, sin=None, n_rot_cols=0, tm=1024, tn=1024):
    bsz, seq, d = x.shape
    n = w.shape[1]
    tm = min(tm, seq)
    in_specs = [
        pl.BlockSpec((1, tm, d), lambda b, i, j: (b, i, 0)),
        pl.BlockSpec((1, d), lambda b, i, j: (0, 0)),
        pl.BlockSpec((1, 1, d), lambda b, i, j: (b, 0, 0)),
        pl.BlockSpec((1, 1, d), lambda b, i, j: (b, 0, 0)),
        pl.BlockSpec((d, tn), lambda b, i, j: (0, j)),
        pl.BlockSpec((1, tn), lambda b, i, j: (0, j)),
    ]
    args = [x, g.reshape(1, d), sc, sh, w, col_scale.reshape(1, n)]
    if epi == "rot":
        in_specs += [pl.BlockSpec((tm, LANES), lambda b, i, j: (i, 0))] * 2
        args += [cos, sin]
    return pl.pallas_call(
        functools.partial(_norm_mm_body, epi=epi, tn=tn, n_rot_tiles=n_rot_cols // tn),
        grid=(bsz, seq // tm, n // tn),
        in_specs=in_specs,
        out_specs=pl.BlockSpec((1, tm, tn), lambda b, i, j: (b, i, j)),
        out_shape=jax.ShapeDtypeStruct((bsz, seq, n), BF16),
        scratch_shapes=[pltpu.VMEM((tm, d), BF16)],
        compiler_params=_params(("arbitrary", "arbitrary", "arbitrary")),
        name="norm_mm_" + epi,
    )(*args)


def _mm_res_body(a_ref, w_ref, x_ref, g_ref, o_ref):
    y = jnp.dot(a_ref[0], w_ref[...], preferred_element_type=F32)
    o_ref[0] = x_ref[0] + g_ref[0] * y


def _mm_res(a, w, x, gate, *, tm=512):
    bsz, seq, k = a.shape
    d = w.shape[1]
    tm = min(tm, seq)
    return pl.pallas_call(
        _mm_res_body,
        grid=(bsz, seq // tm),
        in_specs=[
            pl.BlockSpec((1, tm, k), lambda b, i: (b, i, 0)),
            pl.BlockSpec((k, d), lambda b, i: (0, 0)),
            pl.BlockSpec((1, tm, d), lambda b, i: (b, i, 0)),
            pl.BlockSpec((1, 1, d), lambda b, i: (b, 0, 0)),
        ],
        out_specs=pl.BlockSpec((1, tm, d), lambda b, i: (b, i, 0)),
        out_shape=jax.ShapeDtypeStruct((bsz, seq, d), F32),
        compiler_params=_params(("arbitrary", "arbitrary")),
        name="mm_res",
    )(a, w, x, gate)


def _ret_body(dl_ref, q_ref, k_ref, v_ref, gate_ref, o_ref, st_scr, cb_scr, *, nblk, tb, c, fused):
    t = pl.program_id(2)
    nch = tb // c
    lg = jax.nn.log_sigmoid(dl_ref[0])
    lgf = lg[0:1, 0:1]
    lgb = lg[1:2, 0:1]
    idx = lax.broadcasted_iota(I32, (c, 1), 0).astype(F32)
    tn_dims = (((0,), (0,)), ((), ()))
    nt_dims = (((1,), (1,)), ((), ()))

    @pl.when((t == 0) | (t == nblk))
    def _():
        st_scr[...] = jnp.zeros_like(st_scr)

    @pl.when(t < nblk)
    def _():
        blk = nblk - 1 - t
        xi = jnp.exp(lgb * (c - idx))
        zeta = jnp.exp(lgb * idx)
        g_c = jnp.exp(lgb * c)
        for ci in reversed(range(nch)):
            rows = pl.ds(ci * c, c)
            v = v_ref[0, rows, :]
            if fused:
                qs = q_ref[0, rows, :] * xi.astype(BF16)
                ks = k_ref[0, rows, :] * zeta.astype(BF16)
            else:
                qs = (q_ref[0, rows, :].astype(F32) * xi).astype(BF16)
                ks = (k_ref[0, rows, :].astype(F32) * zeta).astype(BF16)
            st = st_scr[...]
            cross = jnp.dot(qs, st.astype(BF16), preferred_element_type=F32)
            cb_scr[pl.ds(pl.multiple_of(blk * tb + ci * c, c), c), :] = cross.astype(BF16)
            st_scr[...] = st * g_c + lax.dot_general(ks, v, tn_dims, preferred_element_type=F32)

    @pl.when(t >= nblk)
    def _():
        blk = t - nblk
        diff = (lax.broadcasted_iota(I32, (c, c), 0) - lax.broadcasted_iota(I32, (c, c), 1)).astype(F32)
        dmat = jnp.where(diff >= 0, jnp.exp(lgf * jnp.maximum(diff, 0.0)), jnp.exp(lgb * jnp.maximum(-diff, 0.0)))
        xi = jnp.exp(lgf * (idx + 1.0))
        zeta = jnp.exp(lgf * (c - 1.0 - idx))
        g_c = jnp.exp(lgf * c)
        for ci in range(nch):
            rows = pl.ds(ci * c, c)
            qb = q_ref[0, rows, :]
            kb = k_ref[0, rows, :]
            v = v_ref[0, rows, :]
            s = lax.dot_general(qb, kb, nt_dims, preferred_element_type=F32) * dmat
            st = st_scr[...]
            if fused:
                qs = qb * xi.astype(BF16)
                ks = kb * zeta.astype(BF16)
                o = jnp.dot(jnp.concatenate([s.astype(BF16), qs], axis=1),
                            jnp.concatenate([v, st.astype(BF16)], axis=0), preferred_element_type=F32)
            else:
                qs = (qb.astype(F32) * xi).astype(BF16)
                ks = (kb.astype(F32) * zeta).astype(BF16)
                o = jnp.dot(s.astype(BF16), v, preferred_element_type=F32)
                o = o + jnp.dot(qs, st.astype(BF16), preferred_element_type=F32)
            o = o + cb_scr[pl.ds(pl.multiple_of(blk * tb + ci * c, c), c), :].astype(F32)
            st_scr[...] = st * g_c + lax.dot_general(ks, v, tn_dims, preferred_element_type=F32)
            o = o * lax.rsqrt(jnp.mean(o * o, axis=-1, keepdims=True) + EPS)
            o_ref[0, rows, :] = (_silu(gate_ref[0, rows, :].astype(F32)) * o).astype(o_ref.dtype)


def _retention_core(proj, decay_logit, fused):
    bsz, seq, _ = proj.shape
    nh = RET_HEADS
    dk = proj.shape[2] // (6 * nh)
    dv = 2 * dk
    tb = min(RET_BLOCK, seq)
    c = min(RET_CHUNK, tb)
    nblk = seq // tb
    dl = jnp.broadcast_to(decay_logit.T[:, :, None].astype(F32), (nh, 2, LANES))

    def blk_of(t):
        return jnp.where(t < nblk, nblk - 1 - t, t - nblk)

    def fwd_blk(t):
        return jnp.where(t < nblk, 0, t - nblk)

    return pl.pallas_call(
        functools.partial(_ret_body, nblk=nblk, tb=tb, c=c, fused=fused),
        grid=(bsz, nh, 2 * nblk),
        in_specs=[
            pl.BlockSpec((1, 2, LANES), lambda b, h, t: (h, 0, 0)),
            pl.BlockSpec((1, tb, dk), lambda b, h, t: (b, blk_of(t), h)),
            pl.BlockSpec((1, tb, dk), lambda b, h, t: (b, blk_of(t), nh + h)),
            pl.BlockSpec((1, tb, dv), lambda b, h, t: (b, blk_of(t), nh + h)),
            pl.BlockSpec((1, tb, dv), lambda b, h, t: (b, fwd_blk(t), 2 * nh + h)),
        ],
        out_specs=pl.BlockSpec((1, tb, dv), lambda b, h, t: (b, fwd_blk(t), h)),
        out_shape=jax.ShapeDtypeStruct((bsz, seq, nh * dv), BF16),
        scratch_shapes=[pltpu.VMEM((dk, dv), F32), pltpu.VMEM((seq, dv), BF16)],
        compiler_params=_params(("arbitrary", "arbitrary", "arbitrary")),
        name="retention_core",
    )(dl, proj, proj, proj, proj)


def _gmlp_body(u_ref, v_ref, lng_ref, lnb_ref, ws_ref, bs_ref, wo_ref, x_ref, g_ref, o_ref, a_scr, *, tb, gd):
    v = v_ref[0].astype(F32)
    mu = jnp.mean(v, axis=-1, keepdims=True)
    vc = v - mu
    var = jnp.mean(vc * vc, axis=-1, keepdims=True)
    vn = ((vc * lax.rsqrt(var + EPS)) * lng_ref[...] + lnb_ref[...]).astype(BF16)
    for ci in range(tb // GM_CHUNK):
        r0 = ci * GM_CHUNK
        for gi in range(GM_GROUPS):
            c0 = gi * gd
            mixed = jnp.dot(ws_ref[gi], vn[r0:r0 + GM_CHUNK, c0:c0 + gd], preferred_element_type=F32) + bs_ref[gi]
            u = u_ref[0, r0:r0 + GM_CHUNK, c0:c0 + gd].astype(F32)
            a_scr[r0:r0 + GM_CHUNK, c0:c0 + gd] = (u * mixed).astype(BF16)
    y = jnp.dot(a_scr[...], wo_ref[...], preferred_element_type=F32)
    o_ref[0] = x_ref[0] + g_ref[0] * y


def _gmlp_core(z, ln_g, ln_b, w_s, b_s, w_out, x, gate, *, tb=512):
    bsz, seq, ffn = z.shape
    half = ffn // 2
    gd = half // GM_GROUPS
    d = w_out.shape[1]
    tb = min(tb, seq)
    return pl.pallas_call(
        functools.partial(_gmlp_body, tb=tb, gd=gd),
        grid=(bsz, seq // tb),
        in_specs=[
            pl.BlockSpec((1, tb, half), lambda b, i: (b, i, 0)),
            pl.BlockSpec((1, tb, half), lambda b, i: (b, i, 1)),
            pl.BlockSpec((1, half), lambda b, i: (0, 0)),
            pl.BlockSpec((1, half), lambda b, i: (0, 0)),
            pl.BlockSpec((GM_GROUPS, GM_CHUNK, GM_CHUNK), lambda b, i: (0, 0, 0)),
            pl.BlockSpec((GM_GROUPS, GM_CHUNK, 1), lambda b, i: (0, 0, 0)),
            pl.BlockSpec((half, d), lambda b, i: (0, 0)),
            pl.BlockSpec((1, tb, d), lambda b, i: (b, i, 0)),
            pl.BlockSpec((1, 1, d), lambda b, i: (b, 0, 0)),
        ],
        out_specs=pl.BlockSpec((1, tb, d), lambda b, i: (b, i, 0)),
        out_shape=jax.ShapeDtypeStruct((bsz, seq, d), F32),
        scratch_shapes=[pltpu.VMEM((tb, half), BF16)],
        compiler_params=_params(("arbitrary", "arbitrary")),
        name="gmlp_core",
    )(z, z, ln_g.reshape(1, half), ln_b.reshape(1, half), w_s.astype(BF16),
      b_s.reshape(GM_GROUPS, GM_CHUNK, 1), w_out, x, gate)


def _na_body(q_ref, kp_ref, kc_ref, kn_ref, vp_ref, vc_ref, vn_ref, bias_ref, o_ref, kw_scr, vw_scr, s_scr, p_scr,
             *, n_rows, rps, gw, hg, dh):
    i = pl.program_id(2)
    kw_scr[0:rps] = kp_ref[0]
    kw_scr[rps:2 * rps] = kc_ref[0]
    kw_scr[2 * rps:3 * rps] = kn_ref[0]
    vw_scr[0:rps] = vp_ref[0]
    vw_scr[rps:2 * rps] = vc_ref[0]
    vw_scr[2 * rps:3 * rps] = vn_ref[0]
    ch = hg * dh
    nkeys = NA_WIN_ROWS * gw
    head_of_ch = lax.broadcasted_iota(I32, (gw, ch), 1) // dh
    nt_dims = (((1,), (1,)), ((), ()))
    for rl in range(rps):
        r = i * rps + rl
        r0 = jnp.clip(r - NA_WIN_ROWS // 2, 0, n_rows - NA_WIN_ROWS)
        l0 = r0 - i * rps + rps
        ro0 = r0 - r + NA_WIN_ROWS - 1
        q = q_ref[0, rl]
        zero = jnp.zeros_like(q)
        qm = jnp.concatenate([jnp.where(head_of_ch == hh, q, zero) for hh in range(hg)], axis=0)
        kwin = kw_scr[pl.ds(l0, NA_WIN_ROWS)].reshape(nkeys, ch)
        vwin = vw_scr[pl.ds(l0, NA_WIN_ROWS)].reshape(nkeys, ch)
        s_scr[...] = lax.dot_general(qm, kwin, nt_dims, preferred_element_type=F32)
        inv = []
        for hh in range(hg):
            rows = slice(hh * gw, (hh + 1) * gw)
            sh = jnp.concatenate([s_scr[rows, jj * 2 * gw:(jj + 1) * 2 * gw] + bias_ref[ro0 + 2 * jj, 0, rows, :]
                                  for jj in range(NA_WIN_ROWS // 2)], axis=1)
            e = jnp.exp2(sh - jnp.max(sh, axis=-1, keepdims=True))
            inv.append(1.0 / jnp.sum(e, axis=-1, keepdims=True))
            p_scr[rows, :] = e.astype(BF16)
        pv = jnp.dot(p_scr[...], vwin, preferred_element_type=F32)
        out = jnp.zeros((gw, ch), F32)
        for hh in range(hg):
            out = out + jnp.where(head_of_ch == hh, pv[hh * gw:(hh + 1) * gw] * inv[hh], 0.0)
        o_ref[0, rl] = out.astype(o_ref.dtype)


def _na_bias_table(rpb, gw):
    nh = rpb.shape[0]
    kc = NA_WIN_COLS
    col = jnp.arange(gw)
    c0 = jnp.clip(col - kc // 2, 0, gw - kc)
    key = jnp.arange(gw)
    rel = key[None, :] - col[:, None] + kc - 1
    inside = (key[None, :] >= c0[:, None]) & (key[None, :] < c0[:, None] + kc)
    dense = jnp.where(inside[None, None], rpb[:, :, jnp.clip(rel, 0, 2 * kc - 2)].astype(F32) * LOG2_E,
                      NEG_BIAS)
    pair = jnp.concatenate([dense[:, :-1], dense[:, 1:]], axis=-1)
    pair = pair.reshape(nh // NA_HEAD_GROUP, NA_HEAD_GROUP, 2 * NA_WIN_ROWS - 2, gw, 2 * gw)
    return jnp.transpose(pair, (2, 0, 1, 3, 4)).reshape(2 * NA_WIN_ROWS - 2, nh // NA_HEAD_GROUP,
                                                       NA_HEAD_GROUP * gw, 2 * gw)


def _na_core(qkv, rpb):
    bsz, seq, d3 = qkv.shape
    d = d3 // 3
    gw = GRID_W
    n_rows = seq // gw
    rps = NA_ROWS_PER_STEP
    hg = NA_HEAD_GROUP
    dh = d // NA_HEADS
    ch = hg * dh
    ngrp = NA_HEADS // hg
    nb = n_rows // rps
    qkv4 = qkv.reshape(bsz, n_rows, gw, d3)
    bias = _na_bias_table(rpb, gw)
    n_pairs = bias.shape[0]

    def prev(i):
        return jnp.maximum(i - 1, 0)

    def nxt(i):
        return jnp.minimum(i + 1, nb - 1)

    def spec(row_fn, col_off):
        return pl.BlockSpec((1, rps, gw, ch), lambda b, g, i: (b, row_fn(i), 0, col_off + g))

    out = pl.pallas_call(
        functools.partial(_na_body, n_rows=n_rows, rps=rps, gw=gw, hg=hg, dh=dh),
        grid=(bsz, ngrp, nb),
        in_specs=[
            spec(lambda i: i, 0),
            spec(prev, ngrp), spec(lambda i: i, ngrp), spec(nxt, ngrp),
            spec(prev, 2 * ngrp), spec(lambda i: i, 2 * ngrp), spec(nxt, 2 * ngrp),
            pl.BlockSpec((n_pairs, 1, hg * gw, 2 * gw), lambda b, g, i: (0, g, 0, 0)),
        ],
        out_specs=pl.BlockSpec((1, rps, gw, ch), lambda b, g, i: (b, i, 0, g)),
        out_shape=jax.ShapeDtypeStruct((bsz, n_rows, gw, d), BF16),
        scratch_shapes=[pltpu.VMEM((3 * rps, gw, ch), BF16), pltpu.VMEM((3 * rps, gw, ch), BF16),
                        pltpu.VMEM((hg * gw, NA_WIN_ROWS * gw), F32), pltpu.VMEM((hg * gw, NA_WIN_ROWS * gw), BF16)],
        compiler_params=_params(("arbitrary", "arbitrary", "arbitrary")),
        name="na_core",
    )(qkv4, qkv4, qkv4, qkv4, qkv4, qkv4, qkv4, bias)
    return out.reshape(bsz, seq, d)


def _router_body(x_ref, g_ref, sc_ref, sh_ref, rwt_ref, rb_ref, hrow_ref, meta_ref, tot_ref, run_scr, tri_scr,
                 *, tm):
    first = (pl.program_id(0) == 0) & (pl.program_id(1) == 0)

    @pl.when(first)
    def _():
        run_scr[...] = jnp.zeros_like(run_scr)
        earlier = lax.broadcasted_iota(I32, (tm, tm), 0) < lax.broadcasted_iota(I32, (tm, tm), 1)
        tri_scr[...] = earlier.astype(BF16)

    h = _rms_modulate(x_ref[0], g_ref[...], sc_ref[0], sh_ref[0])
    nt_dims = (((1,), (1,)), ((), ()))
    logits = lax.dot_general(rwt_ref[...].astype(BF16), h.astype(BF16), nt_dims,
                             preferred_element_type=F32)
    scores = jax.nn.sigmoid(logits)
    biased = scores + rb_ref[...]
    b = [biased[e:e + 1, :] for e in range(N_EXPERTS)]
    sc = [scores[e:e + 1, :] for e in range(N_EXPERTS)]

    best = None
    grp = jnp.zeros((1, tm), I32)
    for gi in range(N_GROUPS):
        gs = None
        for (pi, pj) in PAIRS:
            ps = b[4 * gi + pi] + b[4 * gi + pj]
            gs = ps if gs is None else jnp.maximum(gs, ps)
        if best is None:
            best = gs
        else:
            better = gs > best
            grp = jnp.where(better, gi, grp)
            best = jnp.where(better, gs, best)

    def pick(vals, kk):
        out = vals[kk]
        for gi in range(1, N_GROUPS):
            out = jnp.where(grp == gi, vals[4 * gi + kk], out)
        return out

    bsel = [pick(b, kk) for kk in range(EXPERTS_PER_GROUP)]
    ssel = [pick(sc, kk) for kk in range(EXPERTS_PER_GROUP)]
    chosen = []
    for kk in range(EXPERTS_PER_GROUP):
        beaten = jnp.zeros((1, tm), I32)
        for jj in range(EXPERTS_PER_GROUP):
            if jj == kk:
                continue
            wins = (bsel[jj] > bsel[kk]) | ((bsel[jj] == bsel[kk]) & (jj < kk))
            beaten = beaten + wins.astype(I32)
        chosen.append(beaten < 2)

    pid = jnp.zeros((1, tm), I32)
    s_lo = jnp.zeros((1, tm), F32)
    s_hi = jnp.zeros((1, tm), F32)
    for pidx, (pi, pj) in enumerate(PAIRS):
        hit = chosen[pi] & chosen[pj]
        pid = jnp.where(hit, pidx, pid)
        s_lo = jnp.where(hit, ssel[pi], s_lo)
        s_hi = jnp.where(hit, ssel[pj], s_hi)
    denom = s_lo + s_hi
    w_lo = s_lo / denom
    w_hi = s_hi / denom
    cls = grp * len(PAIRS) + pid

    onehot = (lax.broadcasted_iota(I32, (CLASS_ROWS, tm), 0) == cls).astype(F32)
    before = jnp.dot(onehot.astype(BF16), tri_scr[...], preferred_element_type=F32)
    run = run_scr[...]
    rank = jnp.sum(onehot * (before + run[:, 0:1]), axis=0, keepdims=True)
    run = run + jnp.sum(onehot, axis=1, keepdims=True)
    run_scr[...] = run
    tot_ref[...] = run

    meta = jnp.concatenate([cls.astype(F32), rank, w_lo, w_hi, w_lo, w_hi, jnp.zeros((2, tm), F32)], axis=0)
    meta_ref[...] = meta
    meta_t = jnp.concatenate([meta, jnp.zeros((LANES - 8, tm), F32)], axis=0).T

    lane = lax.broadcasted_iota(I32, (tm, LANES), 1)
    mbits = lax.bitcast_convert_type(meta_t, I32)
    payload = jnp.where((lane == 2) | (lane == 3), lax.shift_right_logical(mbits, 16),
                        jnp.where((lane == 4) | (lane == 5), mbits & 0xFFFF, 0))
    h0 = lax.bitcast_convert_type(h[:, 0:LANES].astype(BF16).astype(F32), I32)
    hrow_ref[pl.ds(0, tm, stride=ROW_TILES), :] = lax.bitcast_convert_type((h0 & HIGH16) | payload, F32)
    for j in range(1, ROW_TILES):
        hrow_ref[pl.ds(j, tm, stride=ROW_TILES), :] = h[:, j * LANES:(j + 1) * LANES]


def _router(x, g, sc, sh, router_w, router_b):
    bsz, seq, d = x.shape
    tm = min(ROUTER_TILE, seq)
    nt = seq // tm
    return pl.pallas_call(
        functools.partial(_router_body, tm=tm),
        grid=(bsz, nt),
        in_specs=[
            pl.BlockSpec((1, tm, d), lambda b, i: (b, i, 0)),
            pl.BlockSpec((1, d), lambda b, i: (0, 0)),
            pl.BlockSpec((1, 1, d), lambda b, i: (b, 0, 0)),
            pl.BlockSpec((1, 1, d), lambda b, i: (b, 0, 0)),
            pl.BlockSpec((N_EXPERTS, d), lambda b, i: (0, 0)),
            pl.BlockSpec((N_EXPERTS, 1), lambda b, i: (0, 0)),
        ],
        out_specs=[
            pl.BlockSpec((tm * ROW_TILES, LANES), lambda b, i: (b * nt + i, 0)),
            pl.BlockSpec((8, tm), lambda b, i: (0, b * nt + i)),
            pl.BlockSpec((CLASS_ROWS, LANES), lambda b, i: (0, 0)),
        ],
        out_shape=[
            jax.ShapeDtypeStruct((bsz * seq * ROW_TILES, LANES), F32),
            jax.ShapeDtypeStruct((8, bsz * seq), F32),
            jax.ShapeDtypeStruct((CLASS_ROWS, LANES), F32),
        ],
        scratch_shapes=[pltpu.VMEM((CLASS_ROWS, LANES), F32), pltpu.VMEM((tm, tm), BF16)],
        compiler_params=_params(("arbitrary", "arbitrary")),
        name="moe_router",
    )(x, g.reshape(1, d), sc, sh, router_w.T, router_b.reshape(N_EXPERTS, 1))


def _token_copy(src, src_tok, dst, dst_tok, sem):
    s = pl.multiple_of(src_tok * ROW_TILES, ROW_TILES)
    t = pl.multiple_of(dst_tok * ROW_TILES, ROW_TILES)
    return pltpu.make_async_copy(src.at[pl.ds(s, ROW_TILES)], dst.at[pl.ds(t, ROW_TILES)], sem)


def _tile_fill(zero_scr, dst, tile_idx, sem):
    rows = MOE_TILE * ROW_TILES
    return pltpu.make_async_copy(zero_scr, dst.at[pl.ds(pl.multiple_of(tile_idx * rows, rows), rows)], sem)


def _dispatch_body(slot_ref, fill_ref, h_ref, o_ref, zero_scr, sem, *, tb):
    @pl.when(pl.program_id(0) == 0)
    def _():
        zero_scr[...] = jnp.zeros_like(zero_scr)
        for c in range(2 * N_CLASSES):
            @pl.when(fill_ref[c] >= 0)
            def _():
                _tile_fill(zero_scr, o_ref, fill_ref[c], sem).start()
        for c in range(2 * N_CLASSES):
            @pl.when(fill_ref[c] >= 0)
            def _():
                _tile_fill(zero_scr, o_ref, fill_ref[c], sem).wait()

    def start(i, carry):
        _token_copy(h_ref, i, o_ref, slot_ref[i], sem).start()
        return carry

    def wait(i, carry):
        _token_copy(h_ref, i, o_ref, slot_ref[i], sem).wait()
        return carry

    lax.fori_loop(0, tb, start, 0, unroll=8)
    lax.fori_loop(0, tb, wait, 0, unroll=8)


def _dispatch(hrow, slot, fill_tiles, n_slots, *, tb=1024):
    n_tok = hrow.shape[0] // ROW_TILES
    tb = min(tb, n_tok)
    return pl.pallas_call(
        functools.partial(_dispatch_body, tb=tb),
        grid=(n_tok // tb,),
        in_specs=[
            pl.BlockSpec((tb,), lambda i: (i,), memory_space=pltpu.SMEM),
            pl.BlockSpec(memory_space=pltpu.SMEM),
            pl.BlockSpec((tb * ROW_TILES, LANES), lambda i: (i, 0)),
        ],
        out_specs=pl.BlockSpec(memory_space=pl.ANY),
        out_shape=jax.ShapeDtypeStruct((n_slots * ROW_TILES, LANES), hrow.dtype),
        scratch_shapes=[pltpu.VMEM((MOE_TILE * ROW_TILES, LANES), hrow.dtype), pltpu.SemaphoreType.DMA(())],
        compiler_params=_params(("arbitrary",)),
        name="moe_dispatch",
    )(slot, fill_tiles, hrow)


def _expert_body(ea_ref, eb_ref, flags_ref, src_ref, hs_ref, wgua_ref, wgub_ref, wda_ref, wdb_ref, o_ref,
                 wgua_scr, wgub_scr, wda_scr, wdb_scr, *, tile, de):
    del ea_ref, eb_ref, src_ref
    flags = flags_ref[pl.program_id(0)]

    @pl.when((flags & 2) != 0)
    def _():
        wgua_scr[...] = wgua_ref[0].astype(BF16)
        wda_scr[...] = wda_ref[0].astype(BF16)

    @pl.when((flags & 4) != 0)
    def _():
        wgub_scr[...] = wgub_ref[0].astype(BF16)
        wdb_scr[...] = wdb_ref[0].astype(BF16)

    @pl.when((flags & 1) != 0)
    def _():
        w0 = lax.bitcast_convert_type(_token_cols(hs_ref, 0, tile), I32)
        pay = w0 & 0xFFFF
        w_lo = lax.bitcast_convert_type(lax.shift_left(pay[:, 2:3], 16) | pay[:, 4:5], F32)
        w_hi = lax.bitcast_convert_type(lax.shift_left(pay[:, 3:4], 16) | pay[:, 5:6], F32)
        swapped = (flags & 8) != 0
        w_a = jnp.where(swapped, w_hi, w_lo)
        w_b = jnp.where(swapped, w_lo, w_hi)
        cols = [lax.bitcast_convert_type(w0 & HIGH16, F32)]
        cols += [_token_cols(hs_ref, j, tile) for j in range(1, ROW_TILES)]
        h = jnp.concatenate(cols, axis=1).astype(BF16)
        gua = jnp.dot(h, wgua_scr[...], preferred_element_type=F32)
        hida = (_silu(gua[:, 0:de]) * gua[:, de:2 * de]) * w_a
        gub = jnp.dot(h, wgub_scr[...], preferred_element_type=F32)
        hidb = (_silu(gub[:, 0:de]) * gub[:, de:2 * de]) * w_b
        y = jnp.dot(hida.astype(BF16), wda_scr[...], preferred_element_type=F32)
        y = y + jnp.dot(hidb.astype(BF16), wdb_scr[...], preferred_element_type=F32)
        _store_token_tiles(o_ref, y, tile)

    @pl.when((flags & 1) == 0)
    def _():
        o_ref[...] = jnp.zeros_like(o_ref)


def _experts(hs, tile_ea, tile_eb, tile_flags, tile_src, w_gate_up, w_down):
    d = ROW_TILES * LANES
    de = w_down.shape[1]
    n_tiles = hs.shape[0] // (MOE_TILE * ROW_TILES)
    blk = (MOE_TILE * ROW_TILES, LANES)
    grid_spec = pltpu.PrefetchScalarGridSpec(
        num_scalar_prefetch=4,
        grid=(n_tiles,),
        in_specs=[
            pl.BlockSpec(blk, lambda i, ea, eb, fl, src: (src[i], 0)),
            pl.BlockSpec((1, d, 2 * de), lambda i, ea, eb, fl, src: (ea[i], 0, 0)),
            pl.BlockSpec((1, d, 2 * de), lambda i, ea, eb, fl, src: (eb[i], 0, 0)),
            pl.BlockSpec((1, de, d), lambda i, ea, eb, fl, src: (ea[i], 0, 0)),
            pl.BlockSpec((1, de, d), lambda i, ea, eb, fl, src: (eb[i], 0, 0)),
        ],
        out_specs=pl.BlockSpec(blk, lambda i, ea, eb, fl, src: (i, 0)),
        scratch_shapes=[pltpu.VMEM((d, 2 * de), BF16), pltpu.VMEM((d, 2 * de), BF16),
                        pltpu.VMEM((de, d), BF16), pltpu.VMEM((de, d), BF16)],
    )
    return pl.pallas_call(
        functools.partial(_expert_body, tile=MOE_TILE, de=de),
        grid_spec=grid_spec,
        out_shape=jax.ShapeDtypeStruct(hs.shape, F32),
        compiler_params=_params(("arbitrary",)),
        name="moe_experts",
    )(tile_ea, tile_eb, tile_flags, tile_src, hs, w_gate_up, w_gate_up, w_down, w_down)


def _combine_body(slot_ref, nslot_ref, ys_ref, x_ref, g_ref, fg_ref, o_ref, ybuf, sems, *, tb, n_steps, final):
    step = pl.program_id(0) * pl.num_programs(1) + pl.program_id(1)
    cur = step % 2

    def gather(slots, buf, do_start):
        def body(i, carry):
            cp = _token_copy(ys_ref, slots[i], ybuf.at[buf], i, sems.at[buf])
            if do_start:
                cp.start()
            else:
                cp.wait()
            return carry

        lax.fori_loop(0, tb, body, 0, unroll=8)

    @pl.when(step == 0)
    def _():
        gather(slot_ref, 0, True)

    @pl.when(step + 1 < n_steps)
    def _():
        gather(nslot_ref, 1 - cur, True)

    gather(slot_ref, cur, False)
    xn = x_ref[0] + g_ref[0] * _load_token_tiles(ybuf.at[cur], tb)
    if final:
        xn = (xn * lax.rsqrt(jnp.mean(xn * xn, axis=-1, keepdims=True) + EPS)) * fg_ref[...]
    o_ref[0] = xn


def _combine(ys, slot, x, gate, final_g, *, final, tb=512):
    bsz, seq, d = x.shape
    tb = min(tb, seq)
    nt = seq // tb
    n_steps = bsz * nt
    return pl.pallas_call(
        functools.partial(_combine_body, tb=tb, n_steps=n_steps, final=final),
        grid=(bsz, nt),
        in_specs=[
            pl.BlockSpec((tb,), lambda b, i: (b * nt + i,), memory_space=pltpu.SMEM),
            pl.BlockSpec((tb,), lambda b, i: (jnp.minimum(b * nt + i + 1, n_steps - 1),), memory_space=pltpu.SMEM),
            pl.BlockSpec(memory_space=pl.ANY),
            pl.BlockSpec((1, tb, d), lambda b, i: (b, i, 0)),
            pl.BlockSpec((1, 1, d), lambda b, i: (b, 0, 0)),
            pl.BlockSpec((1, d), lambda b, i: (0, 0)),
        ],
        out_specs=pl.BlockSpec((1, tb, d), lambda b, i: (b, i, 0)),
        out_shape=jax.ShapeDtypeStruct((bsz, seq, d), F32),
        scratch_shapes=[pltpu.VMEM((2, tb * ROW_TILES, LANES), F32), pltpu.SemaphoreType.DMA((2,))],
        compiler_params=_params(("arbitrary", "arbitrary")),
        name="moe_combine",
    )(slot, slot, ys, x, gate, final_g.reshape(1, d))


def _moe(x, g, sc, sh, gate, router_w, router_b, w_gate_up, w_down, layer, final_g, final, combine_tb):
    bsz, seq, d = x.shape
    n_tok = bsz * seq
    hrow, meta, tot = _router(x, g, sc, sh, router_w, router_b)

    counts = tot[:N_CLASSES, 0].astype(I32)
    tiles_per_class = (counts + MOE_TILE - 1) // MOE_TILE
    tile_end = jnp.cumsum(tiles_per_class)
    tile_start = tile_end - tiles_per_class
    cls = meta[0].astype(I32)
    rank = meta[1].astype(I32)
    cls_onehot = cls[:, None] == jnp.arange(N_CLASSES, dtype=I32)[None, :]
    slot = jnp.sum(jnp.where(cls_onehot, tile_start[None, :] * MOE_TILE, 0), axis=1) + rank

    n_tiles = n_tok // MOE_TILE + N_CLASSES
    n_slots = n_tiles * MOE_TILE
    tile_id = jnp.arange(n_tiles, dtype=I32)
    tile_valid = (tile_id < tile_end[-1]).astype(I32)
    last_valid = jnp.maximum(tile_end[-1] - 1, 0)
    tile_src = jnp.minimum(tile_id, last_valid)
    tile_cls = jnp.sum((tile_src[:, None] >= tile_end[None, :]).astype(I32), axis=1)
    tile_cls = jnp.minimum(tile_cls, N_CLASSES - 1)
    grp = tile_cls // len(PAIRS)
    pair = tile_cls % len(PAIRS)
    base = layer * N_EXPERTS + grp * EXPERTS_PER_GROUP
    tile_ea = base + jnp.asarray(np.array(PAIR_SLOT_A, np.int32))[pair]
    tile_eb = base + jnp.asarray(np.array(PAIR_SLOT_B, np.int32))[pair]
    swapped = jnp.asarray(np.array([a > b for a, b in zip(PAIR_SLOT_A, PAIR_SLOT_B)], np.int32))[pair]

    def changed(e):
        return jnp.concatenate([jnp.ones((1,), I32), (e[1:] != e[:-1]).astype(I32)])

    tile_flags = tile_valid + 2 * changed(tile_ea) + 4 * changed(tile_eb) + 8 * swapped
    last_tile = jnp.where(tiles_per_class > 0, tile_end - 1, -1)
    unused = tile_end[-1] + jnp.arange(N_CLASSES, dtype=I32)
    fill_tiles = jnp.concatenate([last_tile, jnp.where(unused < n_tiles, unused, -1)])

    hs = _dispatch(hrow, slot, fill_tiles, n_slots)
    ys = _experts(hs, tile_ea, tile_eb, tile_flags, tile_src, w_gate_up, w_down)
    return _combine(ys, slot, x, gate, final_g, final=final, tb=combine_tb)


def _rotary_tables(seq, half):
    inv_freq = RET_ROPE_BASE ** (-jnp.arange(half, dtype=F32) / half)
    ang = jnp.arange(seq, dtype=F32)[:, None] * inv_freq[None, :]
    return jnp.cos(ang), jnp.sin(ang)


def _retention_layer(x, g, sc, sh, gate, w_in, decay_logit, w_out, variant):
    seq = x.shape[1]
    n = w_in.shape[1]
    nq = n // 6
    dk = nq // RET_HEADS
    cos, sin = _rotary_tables(seq, dk // 2)
    col_scale = jnp.concatenate([jnp.ones((nq,), F32), jnp.full((nq,), dk ** -0.5, F32), jnp.ones((n - 2 * nq,), F32)])
    tiles = dict(tm=1024, tn=2048) if variant else dict(tm=1024, tn=1024)
    proj = _norm_mm(x, g, sc, sh, w_in.astype(BF16), col_scale, epi="rot", cos=cos, sin=sin, n_rot_cols=2 * nq,
                    **tiles)
    a = _retention_core(proj, decay_logit, fused=False)
    return _mm_res(a, w_out.astype(BF16), x, gate)


def _gmlp_layer(x, g, sc, sh, gate, w_in, ln_g, ln_b, w_s, b_s, w_out):
    n = w_in.shape[1]
    z = _norm_mm(x, g, sc, sh, w_in.astype(BF16), jnp.ones((n,), F32), epi="gelu")
    return _gmlp_core(z, ln_g, ln_b, w_s, b_s, w_out.astype(BF16), x, gate)


def _na_layer(x, g, sc, sh, gate, w_qkv, rpb, w_out):
    d = x.shape[2]
    dh = d // NA_HEADS
    col_scale = jnp.concatenate([jnp.full((d,), dh ** -0.5 * LOG2_E, F32), jnp.ones((2 * d,), F32)])
    qkv = _norm_mm(x, g, sc, sh, w_qkv.astype(BF16), col_scale)
    a = _na_core(qkv, rpb)
    return _mm_res(a, w_out.astype(BF16), x, gate)


def kernel(x, c, norm_g, final_g, w_ada, b_ada, ret_w_in, ret_decay_logit, ret_w_out, gm_w_in, gm_ln_g, gm_ln_b,
           gm_w_s, gm_b_s, gm_w_out, na_w_qkv, na_rpb, na_w_out, router_w, router_b, moe_w_gate_up, moe_w_down):
    depth = w_ada.shape[0]
    bsz, _, d = x.shape
    mod = _ada(c, w_ada, b_ada)
    w_gate_up_all = moe_w_gate_up.reshape((-1,) + moe_w_gate_up.shape[2:])
    w_down_all = moe_w_down.reshape((-1,) + moe_w_down.shape[2:])
    for i in range(depth):
        kind, j = i % N_MIXERS, i // N_MIXERS
        sh1, sc1, g1, sh2, sc2, g2 = [mod[i, :, k * d:(k + 1) * d].reshape(bsz, 1, d) for k in range(6)]
        if kind == 0:
            x = _retention_layer(x, norm_g[i, 0], sc1, sh1, g1, ret_w_in[j], ret_decay_logit[j], ret_w_out[j],
                                 variant=(j == 0))
        elif kind == 1:
            x = _gmlp_layer(x, norm_g[i, 0], sc1, sh1, g1, gm_w_in[j], gm_ln_g[j], gm_ln_b[j], gm_w_s[j],
                            gm_b_s[j], gm_w_out[j])
        else:
            x = _na_layer(x, norm_g[i, 0], sc1, sh1, g1, na_w_qkv[j], na_rpb[j], na_w_out[j])
        x = _moe(x, norm_g[i, 1], sc2, sh2, g2, router_w, router_b, w_gate_up_all, w_down_all, i, final_g,
                 final=(i == depth - 1), combine_tb=1024 if i < 2 else 512)
    return x
```

```python
import functools

import jax
import jax.numpy as jnp
import numpy as np
from jax import lax
from jax.experimental import pallas as pl
from jax.experimental.pallas import tpu as pltpu

F32 = jnp.float32
BF16 = jnp.bfloat16
I32 = jnp.int32

EPS = 1e-6
N_MIXERS = 3
RET_HEADS = 4
RET_ROPE_BASE = 10000.0
GM_CHUNK = 128
GM_GROUPS = 4
GRID_W = 64
NA_HEADS = 32
NA_WIN_ROWS = 8
NA_WIN_COLS = 16
N_GROUPS = 4
EXPERTS_PER_GROUP = 4
N_EXPERTS = N_GROUPS * EXPERTS_PER_GROUP

LANES = 128
MXU_DIM = 256
VMEM_LIMIT_MIB = 56

RET_CHUNK = 256
RET_BLOCK = 1024
NA_ROWS_PER_STEP = 8
NA_HEAD_GROUP = 8
MOE_TILE = 256
ROUTER_TILE = 512
PAIRS = ((0, 1), (0, 2), (0, 3), (1, 3), (1, 2), (2, 3))
PAIR_SLOT_A = (0, 0, 0, 1, 1, 3)
PAIR_SLOT_B = (1, 2, 3, 3, 2, 2)
N_CLASSES = N_GROUPS * len(PAIRS)
CLASS_ROWS = 32
ROW_TILES = 8
HIGH16 = -65536
NEG_BIAS = -1e30
LOG2_E = float(np.log2(np.e))


def _params(semantics, vmem_mib=VMEM_LIMIT_MIB):
    return pltpu.CompilerParams(dimension_semantics=semantics, vmem_limit_bytes=vmem_mib * 2**20)


def _silu(x):
    return x * jax.nn.sigmoid(x)


def _gelu_exact(x):
    return 0.5 * x * (1.0 + lax.erf(x * np.float32(np.sqrt(0.5))))


def _token_cols(ref, j, n):
    return ref[pl.ds(j, n, stride=ROW_TILES), :]


def _store_token_tiles(dst, val, n):
    for j in range(ROW_TILES):
        dst[pl.ds(j, n, stride=ROW_TILES), :] = val[:, j * LANES:(j + 1) * LANES]


def _load_token_tiles(src, n):
    return jnp.concatenate([_token_cols(src, j, n) for j in range(ROW_TILES)], axis=1)


def _rms_modulate(x, g, sc, sh):
    y = x * lax.rsqrt(jnp.mean(x * x, axis=-1, keepdims=True) + EPS)
    return (y * g) * (1.0 + sc) + sh


def _ada_body(c_ref, w_ref, b_ref, o_ref):
    ca = _silu(c_ref[...]).astype(BF16)
    o_ref[0] = jnp.dot(ca, w_ref[0].astype(BF16), preferred_element_type=F32) + b_ref[0]


def _ada(c, w_ada, b_ada):
    depth, d, n = w_ada.shape
    bsz = c.shape[0]
    rows = 16
    tn = 1536
    c_pad = jnp.zeros((rows, d), F32).at[:bsz].set(c)
    out = pl.pallas_call(
        _ada_body,
        grid=(depth, n // tn),
        in_specs=[
            pl.BlockSpec((rows, d), lambda l, j: (0, 0)),
            pl.BlockSpec((1, d, tn), lambda l, j: (l, 0, j)),
            pl.BlockSpec((1, 1, tn), lambda l, j: (l, 0, j)),
        ],
        out_specs=pl.BlockSpec((1, rows, tn), lambda l, j: (l, 0, j)),
        out_shape=jax.ShapeDtypeStruct((depth, rows, n), F32),
        compiler_params=_params(("arbitrary", "arbitrary")),
        name="ada_mod",
    )(c_pad, w_ada, b_ada.reshape(depth, 1, n))
    return out[:, :bsz]


def _norm_mm_body(*refs, epi, tn, n_rot_tiles):
    if epi == "rot":
        x_ref, g_ref, sc_ref, sh_ref, w_ref, cs_ref, cos_ref, sin_ref, o_ref, h_scr = refs
    else:
        x_ref, g_ref, sc_ref, sh_ref, w_ref, cs_ref, o_ref, h_scr = refs
    j = pl.program_id(2)

    @pl.when(j == 0)
    def _():
        h_scr[...] = _rms_modulate(x_ref[0], g_ref[...], sc_ref[0], sh_ref[0]).astype(BF16)

    acc = jnp.dot(h_scr[...], w_ref[...], preferred_element_type=F32) * cs_ref[...]
    if epi == "gelu":
        o_ref[0] = _gelu_exact(acc).astype(o_ref.dtype)
    elif epi == "rot":
        @pl.when(j < n_rot_tiles)
        def _():
            cos = cos_ref[...]
            sin = sin_ref[...]
            for hh in range(tn // (2 * LANES)):
                lo = hh * 2 * LANES
                x1 = acc[:, lo:lo + LANES]
                x2 = acc[:, lo + LANES:lo + 2 * LANES]
                o_ref[0, :, lo:lo + LANES] = (x1 * cos - x2 * sin).astype(o_ref.dtype)
                o_ref[0, :, lo + LANES:lo + 2 * LANES] = (x1 * sin + x2 * cos).astype(o_ref.dtype)

        @pl.when(j >= n_rot_tiles)
        def _():
            o_ref[0] = acc.astype(o_ref.dtype)
    else:
        o_ref[0] = acc.astype(o_ref.dtype)


def _norm_mm(x, g, sc, sh, w, col_scale, *, epi="none", cos=None, sin=None, n_rot_cols=0, tm=1024, tn=2048):
    bsz, seq, d = x.shape
    n = w.shape[1]
    tm = min(tm, seq)
    in_specs = [
        pl.BlockSpec((1, tm, d), lambda b, i, j: (b, i, 0)),
        pl.BlockSpec((1, d), lambda b, i, j: (0, 0)),
        pl.BlockSpec((1, 1, d), lambda b, i, j: (b, 0, 0)),
        pl.BlockSpec((1, 1, d), lambda b, i, j: (b, 0, 0)),
        pl.BlockSpec((d, tn), lambda b, i, j: (0, j)),
        pl.BlockSpec((1, tn), lambda b, i, j: (0, j)),
    ]
    args = [x, g.reshape(1, d), sc, sh, w, col_scale.reshape(1, n)]
    if epi == "rot":
        in_specs += [pl.BlockSpec((tm, LANES), lambda b, i, j: (i, 0))] * 2
        args += [cos, sin]
    return pl.pallas_call(
        functools.partial(_norm_mm_body, epi=epi, tn=tn, n_rot_tiles=n_rot_cols // tn),
        grid=(bsz, seq // tm, n // tn),
        in_specs=in_specs,
        out_specs=pl.BlockSpec((1, tm, tn), lambda b, i, j: (b, i, j)),
        out_shape=jax.ShapeDtypeStruct((bsz, seq, n), BF16),
        scratch_shapes=[pltpu.VMEM((tm, d), BF16)],
        compiler_params=_params(("arbitrary", "arbitrary", "arbitrary")),
        name="norm_mm_" + epi,
    )(*args)


def _mm_res_body(a_ref, w_ref, x_ref, g_ref, o_ref):
    y = jnp.dot(a_ref[0], w_ref[...], preferred_element_type=F32)
    o_ref[0] = x_ref[0] + g_ref[0] * y


def _mm_res(a, w, x, gate, *, tm=512):
    bsz, seq, k = a.shape
    d = w.shape[1]
    tm = min(tm, seq)
    return pl.pallas_call(
        _mm_res_body,
        grid=(bsz, seq // tm),
        in_specs=[
            pl.BlockSpec((1, tm, k), lambda b, i: (b, i, 0)),
            pl.BlockSpec((k, d), lambda b, i: (0, 0)),
            pl.BlockSpec((1, tm, d), lambda b, i: (b, i, 0)),
            pl.BlockSpec((1, 1, d), lambda b, i: (b, 0, 0)),
        ],
        out_specs=pl.BlockSpec((1, tm, d), lambda b, i: (b, i, 0)),
        out_shape=jax.ShapeDtypeStruct((bsz, seq, d), F32),
        compiler_params=_params(("arbitrary", "arbitrary")),
        name="mm_res",
    )(a, w, x, gate)


def _ret_body(dl_ref, q_ref, k_ref, v_ref, gate_ref, o_ref, st_scr, cb_scr, *, nblk, tb, c):
    t = pl.program_id(2)
    nch = tb // c
    lg = jax.nn.log_sigmoid(dl_ref[0])
    lgf = lg[0:1, 0:1]
    lgb = lg[1:2, 0:1]
    idx = lax.broadcasted_iota(I32, (c, 1), 0).astype(F32)
    tn_dims = (((0,), (0,)), ((), ()))
    nt_dims = (((1,), (1,)), ((), ()))

    @pl.when((t == 0) | (t == nblk))
    def _():
        st_scr[...] = jnp.zeros_like(st_scr)

    @pl.when(t < nblk)
    def _():
        blk = nblk - 1 - t
        xi = jnp.exp(lgb * (c - idx))
        zeta = jnp.exp(lgb * idx)
        g_c = jnp.exp(lgb * c)
        for ci in reversed(range(nch)):
            rows = pl.ds(ci * c, c)
            v = v_ref[0, rows, :]
            qs = (q_ref[0, rows, :].astype(F32) * xi).astype(BF16)
            ks = (k_ref[0, rows, :].astype(F32) * zeta).astype(BF16)
            st = st_scr[...]
            cross = jnp.dot(qs, st.astype(BF16), preferred_element_type=F32)
            cb_scr[pl.ds(pl.multiple_of(blk * tb + ci * c, c), c), :] = cross.astype(BF16)
            st_scr[...] = st * g_c + lax.dot_general(ks, v, tn_dims, preferred_element_type=F32)

    @pl.when(t >= nblk)
    def _():
        blk = t - nblk
        diff = (lax.broadcasted_iota(I32, (c, c), 0) - lax.broadcasted_iota(I32, (c, c), 1)).astype(F32)
        dmat = jnp.where(diff >= 0, jnp.exp(lgf * jnp.maximum(diff, 0.0)), jnp.exp(lgb * jnp.maximum(-diff, 0.0)))
        xi = jnp.exp(lgf * (idx + 1.0))
        zeta = jnp.exp(lgf * (c - 1.0 - idx))
        g_c = jnp.exp(lgf * c)
        for ci in range(nch):
            rows = pl.ds(ci * c, c)
            qb = q_ref[0, rows, :]
            kb = k_ref[0, rows, :]
            v = v_ref[0, rows, :]
            s = lax.dot_general(qb, kb, nt_dims, preferred_element_type=F32) * dmat
            st = st_scr[...]
            qs = (qb.astype(F32) * xi).astype(BF16)
            ks = (kb.astype(F32) * zeta).astype(BF16)
            o = jnp.dot(s.astype(BF16), v, preferred_element_type=F32)
            o = o + jnp.dot(qs, st.astype(BF16), preferred_element_type=F32)
            o = o + cb_scr[pl.ds(pl.multiple_of(blk * tb + ci * c, c), c), :].astype(F32)
            st_scr[...] = st * g_c + lax.dot_general(ks, v, tn_dims, preferred_element_type=F32)
            o = o * lax.rsqrt(jnp.mean(o * o, axis=-1, keepdims=True) + EPS)
            o_ref[0, rows, :] = (_silu(gate_ref[0, rows, :].astype(F32)) * o).astype(o_ref.dtype)


def _retention_core(proj, decay_logit):
    bsz, seq, _ = proj.shape
    nh = RET_HEADS
    dk = proj.shape[2] // (6 * nh)
    dv = 2 * dk
    tb = min(RET_BLOCK, seq)
    c = min(RET_CHUNK, tb)
    nblk = seq // tb
    dl = jnp.broadcast_to(decay_logit.T[:, :, None].astype(F32), (nh, 2, LANES))

    def blk_of(t):
        return jnp.where(t < nblk, nblk - 1 - t, t - nblk)

    def fwd_blk(t):
        return jnp.where(t < nblk, 0, t - nblk)

    return pl.pallas_call(
        functools.partial(_ret_body, nblk=nblk, tb=tb, c=c),
        grid=(bsz, nh, 2 * nblk),
        in_specs=[
            pl.BlockSpec((1, 2, LANES), lambda b, h, t: (h, 0, 0)),
            pl.BlockSpec((1, tb, dk), lambda b, h, t: (b, blk_of(t), h)),
            pl.BlockSpec((1, tb, dk), lambda b, h, t: (b, blk_of(t), nh + h)),
            pl.BlockSpec((1, tb, dv), lambda b, h, t: (b, blk_of(t), nh + h)),
            pl.BlockSpec((1, tb, dv), lambda b, h, t: (b, fwd_blk(t), 2 * nh + h)),
        ],
        out_specs=pl.BlockSpec((1, tb, dv), lambda b, h, t: (b, fwd_blk(t), h)),
        out_shape=jax.ShapeDtypeStruct((bsz, seq, nh * dv), BF16),
        scratch_shapes=[pltpu.VMEM((dk, dv), F32), pltpu.VMEM((seq, dv), BF16)],
        compiler_params=_params(("arbitrary", "arbitrary", "arbitrary")),
        name="retention_core",
    )(dl, proj, proj, proj, proj)


def _gmlp_body(u_ref, v_ref, lng_ref, lnb_ref, ws_ref, bs_ref, wo_ref, x_ref, g_ref, o_ref, a_scr, *, tb, gd):
    v = v_ref[0].astype(F32)
    mu = jnp.mean(v, axis=-1, keepdims=True)
    vc = v - mu
    var = jnp.mean(vc * vc, axis=-1, keepdims=True)
    vn = ((vc * lax.rsqrt(var + EPS)) * lng_ref[...] + lnb_ref[...]).astype(BF16)
    for ci in range(tb // GM_CHUNK):
        r0 = ci * GM_CHUNK
        for gi in range(GM_GROUPS):
            c0 = gi * gd
            mixed = jnp.dot(ws_ref[gi], vn[r0:r0 + GM_CHUNK, c0:c0 + gd], preferred_element_type=F32) + bs_ref[gi]
            u = u_ref[0, r0:r0 + GM_CHUNK, c0:c0 + gd].astype(F32)
            a_scr[r0:r0 + GM_CHUNK, c0:c0 + gd] = (u * mixed).astype(BF16)
    y = jnp.dot(a_scr[...], wo_ref[...], preferred_element_type=F32)
    o_ref[0] = x_ref[0] + g_ref[0] * y


def _gmlp_core(z, ln_g, ln_b, w_s, b_s, w_out, x, gate, *, tb=512):
    bsz, seq, ffn = z.shape
    half = ffn // 2
    gd = half // GM_GROUPS
    d = w_out.shape[1]
    tb = min(tb, seq)
    return pl.pallas_call(
        functools.partial(_gmlp_body, tb=tb, gd=gd),
        grid=(bsz, seq // tb),
        in_specs=[
            pl.BlockSpec((1, tb, half), lambda b, i: (b, i, 0)),
            pl.BlockSpec((1, tb, half), lambda b, i: (b, i, 1)),
            pl.BlockSpec((1, half), lambda b, i: (0, 0)),
            pl.BlockSpec((1, half), lambda b, i: (0, 0)),
            pl.BlockSpec((GM_GROUPS, GM_CHUNK, GM_CHUNK), lambda b, i: (0, 0, 0)),
            pl.BlockSpec((GM_GROUPS, GM_CHUNK, 1), lambda b, i: (0, 0, 0)),
            pl.BlockSpec((half, d), lambda b, i: (0, 0)),
            pl.BlockSpec((1, tb, d), lambda b, i: (b, i, 0)),
            pl.BlockSpec((1, 1, d), lambda b, i: (b, 0, 0)),
        ],
        out_specs=pl.BlockSpec((1, tb, d), lambda b, i: (b, i, 0)),
        out_shape=jax.ShapeDtypeStruct((bsz, seq, d), F32),
        scratch_shapes=[pltpu.VMEM((tb, half), BF16)],
        compiler_params=_params(("arbitrary", "arbitrary")),
        name="gmlp_core",
    )(z, z, ln_g.reshape(1, half), ln_b.reshape(1, half), w_s.astype(BF16),
      b_s.reshape(GM_GROUPS, GM_CHUNK, 1), w_out, x, gate)


def _na_body(q_ref, kp_ref, kc_ref, kn_ref, vp_ref, vc_ref, vn_ref, bias_ref, o_ref, kw_scr, vw_scr, s_scr, p_scr,
             *, n_rows, rps, gw, hg, dh):
    i = pl.program_id(2)
    kw_scr[0:rps] = kp_ref[0]
    kw_scr[rps:2 * rps] = kc_ref[0]
    kw_scr[2 * rps:3 * rps] = kn_ref[0]
    vw_scr[0:rps] = vp_ref[0]
    vw_scr[rps:2 * rps] = vc_ref[0]
    vw_scr[2 * rps:3 * rps] = vn_ref[0]
    ch = hg * dh
    nkeys = NA_WIN_ROWS * gw
    head_of_ch = lax.broadcasted_iota(I32, (gw, ch), 1) // dh
    nt_dims = (((1,), (1,)), ((), ()))
    for rl in range(rps):
        r = i * rps + rl
        r0 = jnp.clip(r - NA_WIN_ROWS // 2, 0, n_rows - NA_WIN_ROWS)
        l0 = r0 - i * rps + rps
        ro0 = r0 - r + NA_WIN_ROWS - 1
        q = q_ref[0, rl]
        zero = jnp.zeros_like(q)
        qm = jnp.concatenate([jnp.where(head_of_ch == hh, q, zero) for hh in range(hg)], axis=0)
        kwin = kw_scr[pl.ds(l0, NA_WIN_ROWS)].reshape(nkeys, ch)
        vwin = vw_scr[pl.ds(l0, NA_WIN_ROWS)].reshape(nkeys, ch)
        s_scr[...] = lax.dot_general(qm, kwin, nt_dims, preferred_element_type=F32)
        inv = []
        for hh in range(hg):
            rows = slice(hh * gw, (hh + 1) * gw)
            sh = jnp.concatenate([s_scr[rows, jj * 2 * gw:(jj + 1) * 2 * gw] + bias_ref[ro0 + 2 * jj, 0, rows, :]
                                  for jj in range(NA_WIN_ROWS // 2)], axis=1)
            e = jnp.exp2(sh - jnp.max(sh, axis=-1, keepdims=True))
            inv.append(1.0 / jnp.sum(e, axis=-1, keepdims=True))
            p_scr[rows, :] = e.astype(BF16)
        pv = jnp.dot(p_scr[...], vwin, preferred_element_type=F32)
        out = jnp.zeros((gw, ch), F32)
        for hh in range(hg):
            out = out + jnp.where(head_of_ch == hh, pv[hh * gw:(hh + 1) * gw] * inv[hh], 0.0)
        o_ref[0, rl] = out.astype(o_ref.dtype)


def _na_bias_table(rpb, gw):
    nh = rpb.shape[0]
    kc = NA_WIN_COLS
    col = jnp.arange(gw)
    c0 = jnp.clip(col - kc // 2, 0, gw - kc)
    key = jnp.arange(gw)
    rel = key[None, :] - col[:, None] + kc - 1
    inside = (key[None, :] >= c0[:, None]) & (key[None, :] < c0[:, None] + kc)
    dense = jnp.where(inside[None, None], rpb[:, :, jnp.clip(rel, 0, 2 * kc - 2)].astype(F32) * LOG2_E,
                      NEG_BIAS)
    pair = jnp.concatenate([dense[:, :-1], dense[:, 1:]], axis=-1)
    pair = pair.reshape(nh // NA_HEAD_GROUP, NA_HEAD_GROUP, 2 * NA_WIN_ROWS - 2, gw, 2 * gw)
    return jnp.transpose(pair, (2, 0, 1, 3, 4)).reshape(2 * NA_WIN_ROWS - 2, nh // NA_HEAD_GROUP,
                                                       NA_HEAD_GROUP * gw, 2 * gw)


def _na_core(qkv, rpb):
    bsz, seq, d3 = qkv.shape
    d = d3 // 3
    gw = GRID_W
    n_rows = seq // gw
    rps = NA_ROWS_PER_STEP
    hg = NA_HEAD_GROUP
    dh = d // NA_HEADS
    ch = hg * dh
    ngrp = NA_HEADS // hg
    nb = n_rows // rps
    qkv4 = qkv.reshape(bsz, n_rows, gw, d3)
    bias = _na_bias_table(rpb, gw)
    n_pairs = bias.shape[0]

    def prev(i):
        return jnp.maximum(i - 1, 0)

    def nxt(i):
        return jnp.minimum(i + 1, nb - 1)

    def spec(row_fn, col_off):
        return pl.BlockSpec((1, rps, gw, ch), lambda b, g, i: (b, row_fn(i), 0, col_off + g))

    out = pl.pallas_call(
        functools.partial(_na_body, n_rows=n_rows, rps=rps, gw=gw, hg=hg, dh=dh),
        grid=(bsz, ngrp, nb),
        in_specs=[
            spec(lambda i: i, 0),
            spec(prev, ngrp), spec(lambda i: i, ngrp), spec(nxt, ngrp),
            spec(prev, 2 * ngrp), spec(lambda i: i, 2 * ngrp), spec(nxt, 2 * ngrp),
            pl.BlockSpec((n_pairs, 1, hg * gw, 2 * gw), lambda b, g, i: (0, g, 0, 0)),
        ],
        out_specs=pl.BlockSpec((1, rps, gw, ch), lambda b, g, i: (b, i, 0, g)),
        out_shape=jax.ShapeDtypeStruct((bsz, n_rows, gw, d), BF16),
        scratch_shapes=[pltpu.VMEM((3 * rps, gw, ch), BF16), pltpu.VMEM((3 * rps, gw, ch), BF16),
                        pltpu.VMEM((hg * gw, NA_WIN_ROWS * gw), F32), pltpu.VMEM((hg * gw, NA_WIN_ROWS * gw), BF16)],
        compiler_params=_params(("arbitrary", "arbitrary", "arbitrary")),
        name="na_core",
    )(qkv4, qkv4, qkv4, qkv4, qkv4, qkv4, qkv4, bias)
    return out.reshape(bsz, seq, d)


def _router_body(x_ref, g_ref, sc_ref, sh_ref, rwt_ref, rb_ref, hrow_ref, meta_ref, tot_ref, run_scr, tri_scr,
                 *, tm):
    first = (pl.program_id(0) == 0) & (pl.program_id(1) == 0)

    @pl.when(first)
    def _():
        run_scr[...] = jnp.zeros_like(run_scr)
        earlier = lax.broadcasted_iota(I32, (tm, tm), 0) < lax.broadcasted_iota(I32, (tm, tm), 1)
        tri_scr[...] = earlier.astype(BF16)

    h = _rms_modulate(x_ref[0], g_ref[...], sc_ref[0], sh_ref[0])
    nt_dims = (((1,), (1,)), ((), ()))
    logits = lax.dot_general(rwt_ref[...].astype(BF16), h.astype(BF16), nt_dims,
                             preferred_element_type=F32)
    scores = jax.nn.sigmoid(logits)
    biased = scores + rb_ref[...]
    b = [biased[e:e + 1, :] for e in range(N_EXPERTS)]
    sc = [scores[e:e + 1, :] for e in range(N_EXPERTS)]

    best = None
    grp = jnp.zeros((1, tm), I32)
    for gi in range(N_GROUPS):
        gs = None
        for (pi, pj) in PAIRS:
            ps = b[4 * gi + pi] + b[4 * gi + pj]
            gs = ps if gs is None else jnp.maximum(gs, ps)
        if best is None:
            best = gs
        else:
            better = gs > best
            grp = jnp.where(better, gi, grp)
            best = jnp.where(better, gs, best)

    def pick(vals, kk):
        out = vals[kk]
        for gi in range(1, N_GROUPS):
            out = jnp.where(grp == gi, vals[4 * gi + kk], out)
        return out

    bsel = [pick(b, kk) for kk in range(EXPERTS_PER_GROUP)]
    ssel = [pick(sc, kk) for kk in range(EXPERTS_PER_GROUP)]
    chosen = []
    for kk in range(EXPERTS_PER_GROUP):
        beaten = jnp.zeros((1, tm), I32)
        for jj in range(EXPERTS_PER_GROUP):
            if jj == kk:
                continue
            wins = (bsel[jj] > bsel[kk]) | ((bsel[jj] == bsel[kk]) & (jj < kk))
            beaten = beaten + wins.astype(I32)
        chosen.append(beaten < 2)

    pid = jnp.zeros((1, tm), I32)
    s_lo = jnp.zeros((1, tm), F32)
    s_hi = jnp.zeros((1, tm), F32)
    for pidx, (pi, pj) in enumerate(PAIRS):
        hit = chosen[pi] & chosen[pj]
        pid = jnp.where(hit, pidx, pid)
        s_lo = jnp.where(hit, ssel[pi], s_lo)
        s_hi = jnp.where(hit, ssel[pj], s_hi)
    denom = s_lo + s_hi
    w_lo = s_lo / denom
    w_hi = s_hi / denom
    cls = grp * len(PAIRS) + pid

    onehot = (lax.broadcasted_iota(I32, (CLASS_ROWS, tm), 0) == cls).astype(F32)
    before = jnp.dot(onehot.astype(BF16), tri_scr[...], preferred_element_type=F32)
    run = run_scr[...]
    rank = jnp.sum(onehot * (before + run[:, 0:1]), axis=0, keepdims=True)
    run = run + jnp.sum(onehot, axis=1, keepdims=True)
    run_scr[...] = run
    tot_ref[...] = run

    meta = jnp.concatenate([cls.astype(F32), rank, w_lo, w_hi, w_lo, w_hi, jnp.zeros((2, tm), F32)], axis=0)
    meta_ref[...] = meta
    meta_t = jnp.concatenate([meta, jnp.zeros((LANES - 8, tm), F32)], axis=0).T

    lane = lax.broadcasted_iota(I32, (tm, LANES), 1)
    mbits = lax.bitcast_convert_type(meta_t, I32)
    payload = jnp.where((lane == 2) | (lane == 3), lax.shift_right_logical(mbits, 16),
                        jnp.where((lane == 4) | (lane == 5), mbits & 0xFFFF, 0))
    h0 = lax.bitcast_convert_type(h[:, 0:LANES].astype(BF16).astype(F32), I32)
    hrow_ref[pl.ds(0, tm, stride=ROW_TILES), :] = lax.bitcast_convert_type((h0 & HIGH16) | payload, F32)
    for j in range(1, ROW_TILES):
        hrow_ref[pl.ds(j, tm, stride=ROW_TILES), :] = h[:, j * LANES:(j + 1) * LANES]


def _router(x, g, sc, sh, router_w, router_b):
    bsz, seq, d = x.shape
    tm = min(ROUTER_TILE, seq)
    nt = seq // tm
    return pl.pallas_call(
        functools.partial(_router_body, tm=tm),
        grid=(bsz, nt),
        in_specs=[
            pl.BlockSpec((1, tm, d), lambda b, i: (b, i, 0)),
            pl.BlockSpec((1, d), lambda b, i: (0, 0)),
            pl.BlockSpec((1, 1, d), lambda b, i: (b, 0, 0)),
            pl.BlockSpec((1, 1, d), lambda b, i: (b, 0, 0)),
            pl.BlockSpec((N_EXPERTS, d), lambda b, i: (0, 0)),
            pl.BlockSpec((N_EXPERTS, 1), lambda b, i: (0, 0)),
        ],
        out_specs=[
            pl.BlockSpec((tm * ROW_TILES, LANES), lambda b, i: (b * nt + i, 0)),
            pl.BlockSpec((8, tm), lambda b, i: (0, b * nt + i)),
            pl.BlockSpec((CLASS_ROWS, LANES), lambda b, i: (0, 0)),
        ],
        out_shape=[
            jax.ShapeDtypeStruct((bsz * seq * ROW_TILES, LANES), F32),
            jax.ShapeDtypeStruct((8, bsz * seq), F32),
            jax.ShapeDtypeStruct((CLASS_ROWS, LANES), F32),
        ],
        scratch_shapes=[pltpu.VMEM((CLASS_ROWS, LANES), F32), pltpu.VMEM((tm, tm), BF16)],
        compiler_params=_params(("arbitrary", "arbitrary")),
        name="moe_router",
    )(x, g.reshape(1, d), sc, sh, router_w.T, router_b.reshape(N_EXPERTS, 1))


def _token_copy(src, src_tok, dst, dst_tok, sem):
    s = pl.multiple_of(src_tok * ROW_TILES, ROW_TILES)
    t = pl.multiple_of(dst_tok * ROW_TILES, ROW_TILES)
    return pltpu.make_async_copy(src.at[pl.ds(s, ROW_TILES)], dst.at[pl.ds(t, ROW_TILES)], sem)


def _tile_fill(zero_scr, dst, tile_idx, sem):
    rows = MOE_TILE * ROW_TILES
    return pltpu.make_async_copy(zero_scr, dst.at[pl.ds(pl.multiple_of(tile_idx * rows, rows), rows)], sem)


def _dispatch_body(slot_ref, fill_ref, h_ref, o_ref, zero_scr, sem, *, tb):
    @pl.when(pl.program_id(0) == 0)
    def _():
        zero_scr[...] = jnp.zeros_like(zero_scr)
        for c in range(2 * N_CLASSES):
            @pl.when(fill_ref[c] >= 0)
            def _():
                _tile_fill(zero_scr, o_ref, fill_ref[c], sem).start()
        for c in range(2 * N_CLASSES):
            @pl.when(fill_ref[c] >= 0)
            def _():
                _tile_fill(zero_scr, o_ref, fill_ref[c], sem).wait()

    def start(i, carry):
        _token_copy(h_ref, i, o_ref, slot_ref[i], sem).start()
        return carry

    def wait(i, carry):
        _token_copy(h_ref, i, o_ref, slot_ref[i], sem).wait()
        return carry

    lax.fori_loop(0, tb, start, 0, unroll=8)
    lax.fori_loop(0, tb, wait, 0, unroll=8)


def _dispatch(hrow, slot, fill_tiles, n_slots, *, tb=2048):
    n_tok = hrow.shape[0] // ROW_TILES
    tb = min(tb, n_tok)
    return pl.pallas_call(
        functools.partial(_dispatch_body, tb=tb),
        grid=(n_tok // tb,),
        in_specs=[
            pl.BlockSpec((tb,), lambda i: (i,), memory_space=pltpu.SMEM),
            pl.BlockSpec(memory_space=pltpu.SMEM),
            pl.BlockSpec((tb * ROW_TILES, LANES), lambda i: (i, 0)),
        ],
        out_specs=pl.BlockSpec(memory_space=pl.ANY),
        out_shape=jax.ShapeDtypeStruct((n_slots * ROW_TILES, LANES), hrow.dtype),
        scratch_shapes=[pltpu.VMEM((MOE_TILE * ROW_TILES, LANES), hrow.dtype), pltpu.SemaphoreType.DMA(())],
        compiler_params=_params(("arbitrary",)),
        name="moe_dispatch",
    )(slot, fill_tiles, hrow)


def _expert_body(ea_ref, eb_ref, flags_ref, src_ref, hs_ref, wgua_ref, wgub_ref, wda_ref, wdb_ref, o_ref,
                 wgua_scr, wgub_scr, wda_scr, wdb_scr, *, tile, de):
    del ea_ref, eb_ref, src_ref
    flags = flags_ref[pl.program_id(0)]

    @pl.when((flags & 2) != 0)
    def _():
        wgua_scr[...] = wgua_ref[0].astype(BF16)
        wda_scr[...] = wda_ref[0].astype(BF16)

    @pl.when((flags & 4) != 0)
    def _():
        wgub_scr[...] = wgub_ref[0].astype(BF16)
        wdb_scr[...] = wdb_ref[0].astype(BF16)

    @pl.when((flags & 1) != 0)
    def _():
        w0 = lax.bitcast_convert_type(_token_cols(hs_ref, 0, tile), I32)
        pay = w0 & 0xFFFF
        w_lo = lax.bitcast_convert_type(lax.shift_left(pay[:, 2:3], 16) | pay[:, 4:5], F32)
        w_hi = lax.bitcast_convert_type(lax.shift_left(pay[:, 3:4], 16) | pay[:, 5:6], F32)
        swapped = (flags & 8) != 0
        w_a = jnp.where(swapped, w_hi, w_lo)
        w_b = jnp.where(swapped, w_lo, w_hi)
        cols = [lax.bitcast_convert_type(w0 & HIGH16, F32)]
        cols += [_token_cols(hs_ref, j, tile) for j in range(1, ROW_TILES)]
        h = jnp.concatenate(cols, axis=1).astype(BF16)
        gua = jnp.dot(h, wgua_scr[...], preferred_element_type=F32)
        hida = (_silu(gua[:, 0:de]) * gua[:, de:2 * de]) * w_a
        gub = jnp.dot(h, wgub_scr[...], preferred_element_type=F32)
        hidb = (_silu(gub[:, 0:de]) * gub[:, de:2 * de]) * w_b
        y = jnp.dot(hida.astype(BF16), wda_scr[...], preferred_element_type=F32)
        y = y + jnp.dot(hidb.astype(BF16), wdb_scr[...], preferred_element_type=F32)
        _store_token_tiles(o_ref, y, tile)

    @pl.when((flags & 1) == 0)
    def _():
        o_ref[...] = jnp.zeros_like(o_ref)


def _experts(hs, tile_ea, tile_eb, tile_flags, tile_src, w_gate_up, w_down):
    d = ROW_TILES * LANES
    de = w_down.shape[1]
    n_tiles = hs.shape[0] // (MOE_TILE * ROW_TILES)
    blk = (MOE_TILE * ROW_TILES, LANES)
    grid_spec = pltpu.PrefetchScalarGridSpec(
        num_scalar_prefetch=4,
        grid=(n_tiles,),
        in_specs=[
            pl.BlockSpec(blk, lambda i, ea, eb, fl, src: (src[i], 0)),
            pl.BlockSpec((1, d, 2 * de), lambda i, ea, eb, fl, src: (ea[i], 0, 0)),
            pl.BlockSpec((1, d, 2 * de), lambda i, ea, eb, fl, src: (eb[i], 0, 0)),
            pl.BlockSpec((1, de, d), lambda i, ea, eb, fl, src: (ea[i], 0, 0)),
            pl.BlockSpec((1, de, d), lambda i, ea, eb, fl, src: (eb[i], 0, 0)),
        ],
        out_specs=pl.BlockSpec(blk, lambda i, ea, eb, fl, src: (i, 0)),
        scratch_shapes=[pltpu.VMEM((d, 2 * de), BF16), pltpu.VMEM((d, 2 * de), BF16),
                        pltpu.VMEM((de, d), BF16), pltpu.VMEM((de, d), BF16)],
    )
    return pl.pallas_call(
        functools.partial(_expert_body, tile=MOE_TILE, de=de),
        grid_spec=grid_spec,
        out_shape=jax.ShapeDtypeStruct(hs.shape, F32),
        compiler_params=_params(("arbitrary",)),
        name="moe_experts",
    )(tile_ea, tile_eb, tile_flags, tile_src, hs, w_gate_up, w_gate_up, w_down, w_down)


def _combine_body(slot_ref, nslot_ref, ys_ref, x_ref, g_ref, fg_ref, o_ref, ybuf, sems, *, tb, n_steps, final):
    step = pl.program_id(0) * pl.num_programs(1) + pl.program_id(1)
    cur = step % 2

    def gather(slots, buf, do_start):
        def body(i, carry):
            cp = _token_copy(ys_ref, slots[i], ybuf.at[buf], i, sems.at[buf])
            if do_start:
                cp.start()
            else:
                cp.wait()
            return carry

        lax.fori_loop(0, tb, body, 0, unroll=8)

    @pl.when(step == 0)
    def _():
        gather(slot_ref, 0, True)

    @pl.when(step + 1 < n_steps)
    def _():
        gather(nslot_ref, 1 - cur, True)

    gather(slot_ref, cur, False)
    xn = x_ref[0] + g_ref[0] * _load_token_tiles(ybuf.at[cur], tb)
    if final:
        xn = (xn * lax.rsqrt(jnp.mean(xn * xn, axis=-1, keepdims=True) + EPS)) * fg_ref[...]
    o_ref[0] = xn


def _combine(ys, slot, x, gate, final_g, *, final, tb=512):
    bsz, seq, d = x.shape
    tb = min(tb, seq)
    nt = seq // tb
    n_steps = bsz * nt
    return pl.pallas_call(
        functools.partial(_combine_body, tb=tb, n_steps=n_steps, final=final),
        grid=(bsz, nt),
        in_specs=[
            pl.BlockSpec((tb,), lambda b, i: (b * nt + i,), memory_space=pltpu.SMEM),
            pl.BlockSpec((tb,), lambda b, i: (jnp.minimum(b * nt + i + 1, n_steps - 1),), memory_space=pltpu.SMEM),
            pl.BlockSpec(memory_space=pl.ANY),
            pl.BlockSpec((1, tb, d), lambda b, i: (b, i, 0)),
            pl.BlockSpec((1, 1, d), lambda b, i: (b, 0, 0)),
            pl.BlockSpec((1, d), lambda b, i: (0, 0)),
        ],
        out_specs=pl.BlockSpec((1, tb, d), lambda b, i: (b, i, 0)),
        out_shape=jax.ShapeDtypeStruct((bsz, seq, d), F32),
        scratch_shapes=[pltpu.VMEM((2, tb * ROW_TILES, LANES), F32), pltpu.SemaphoreType.DMA((2,))],
        compiler_params=_params(("arbitrary", "arbitrary")),
        name="moe_combine",
    )(slot, slot, ys, x, gate, final_g.reshape(1, d))


def _moe(x, g, sc, sh, gate, router_w, router_b, w_gate_up, w_down, layer, final_g, final):
    bsz, seq, d = x.shape
    n_tok = bsz * seq
    hrow, meta, tot = _router(x, g, sc, sh, router_w, router_b)

    counts = tot[:N_CLASSES, 0].astype(I32)
    tiles_per_class = (counts + MOE_TILE - 1) // MOE_TILE
    tile_end = jnp.cumsum(tiles_per_class)
    tile_start = tile_end - tiles_per_class
    cls = meta[0].astype(I32)
    rank = meta[1].astype(I32)
    cls_onehot = cls[:, None] == jnp.arange(N_CLASSES, dtype=I32)[None, :]
    slot = jnp.sum(jnp.where(cls_onehot, tile_start[None, :] * MOE_TILE, 0), axis=1) + rank

    n_tiles = n_tok // MOE_TILE + N_CLASSES
    n_slots = n_tiles * MOE_TILE
    tile_id = jnp.arange(n_tiles, dtype=I32)
    tile_valid = (tile_id < tile_end[-1]).astype(I32)
    last_valid = jnp.maximum(tile_end[-1] - 1, 0)
    tile_src = jnp.minimum(tile_id, last_valid)
    tile_cls = jnp.sum((tile_src[:, None] >= tile_end[None, :]).astype(I32), axis=1)
    tile_cls = jnp.minimum(tile_cls, N_CLASSES - 1)
    grp = tile_cls // len(PAIRS)
    pair = tile_cls % len(PAIRS)
    base = layer * N_EXPERTS + grp * EXPERTS_PER_GROUP
    tile_ea = base + jnp.asarray(np.array(PAIR_SLOT_A, np.int32))[pair]
    tile_eb = base + jnp.asarray(np.array(PAIR_SLOT_B, np.int32))[pair]
    swapped = jnp.asarray(np.array([a > b for a, b in zip(PAIR_SLOT_A, PAIR_SLOT_B)], np.int32))[pair]

    def changed(e):
        return jnp.concatenate([jnp.ones((1,), I32), (e[1:] != e[:-1]).astype(I32)])

    tile_flags = tile_valid + 2 * changed(tile_ea) + 4 * changed(tile_eb) + 8 * swapped
    last_tile = jnp.where(tiles_per_class > 0, tile_end - 1, -1)
    unused = tile_end[-1] + jnp.arange(N_CLASSES, dtype=I32)
    fill_tiles = jnp.concatenate([last_tile, jnp.where(unused < n_tiles, unused, -1)])

    hs = _dispatch(hrow, slot, fill_tiles, n_slots)
    ys = _experts(hs, tile_ea, tile_eb, tile_flags, tile_src, w_gate_up, w_down)
    return _combine(ys, slot, x, gate, final_g, final=final)


def _rotary_tables(seq, half):
    inv_freq = RET_ROPE_BASE ** (-jnp.arange(half, dtype=F32) / half)
    ang = jnp.arange(seq, dtype=F32)[:, None] * inv_freq[None, :]
    return jnp.cos(ang), jnp.sin(ang)


def _retention_layer(x, g, sc, sh, gate, w_in, decay_logit, w_out):
    seq = x.shape[1]
    n = w_in.shape[1]
    nq = n // 6
    dk = nq // RET_HEADS
    cos, sin = _rotary_tables(seq, dk // 2)
    col_scale = jnp.concatenate([jnp.ones((nq,), F32), jnp.full((nq,), dk ** -0.5, F32), jnp.ones((n - 2 * nq,), F32)])
    proj = _norm_mm(x, g, sc, sh, w_in.astype(BF16), col_scale, epi="rot", cos=cos, sin=sin, n_rot_cols=2 * nq)
    a = _retention_core(proj, decay_logit)
    return _mm_res(a, w_out.astype(BF16), x, gate)


def _gmlp_layer(x, g, sc, sh, gate, w_in, ln_g, ln_b, w_s, b_s, w_out):
    n = w_in.shape[1]
    z = _norm_mm(x, g, sc, sh, w_in.astype(BF16), jnp.ones((n,), F32), epi="gelu")
    return _gmlp_core(z, ln_g, ln_b, w_s, b_s, w_out.astype(BF16), x, gate)


def _na_layer(x, g, sc, sh, gate, w_qkv, rpb, w_out):
    d = x.shape[2]
    dh = d // NA_HEADS
    col_scale = jnp.concatenate([jnp.full((d,), dh ** -0.5 * LOG2_E, F32), jnp.ones((2 * d,), F32)])
    qkv = _norm_mm(x, g, sc, sh, w_qkv.astype(BF16), col_scale, tn=3 * d // 2)
    a = _na_core(qkv, rpb)
    return _mm_res(a, w_out.astype(BF16), x, gate)


def kernel(x, c, norm_g, final_g, w_ada, b_ada, ret_w_in, ret_decay_logit, ret_w_out, gm_w_in, gm_ln_g, gm_ln_b,
           gm_w_s, gm_b_s, gm_w_out, na_w_qkv, na_rpb, na_w_out, router_w, router_b, moe_w_gate_up, moe_w_down):
    depth = w_ada.shape[0]
    bsz, _, d = x.shape
    mod = _ada(c, w_ada, b_ada)
    w_gate_up_all = moe_w_gate_up.reshape((-1,) + moe_w_gate_up.shape[2:])
    w_down_all = moe_w_down.reshape((-1,) + moe_w_down.shape[2:])
    for i in range(depth):
        kind, j = i % N_MIXERS, i // N_MIXERS
        sh1, sc1, g1, sh2, sc2, g2 = [mod[i, :, k * d:(k + 1) * d].reshape(bsz, 1, d) for k in range(6)]
        if kind == 0:
            x = _retention_layer(x, norm_g[i, 0], sc1, sh1, g1, ret_w_in[j], ret_decay_logit[j], ret_w_out[j])
        elif kind == 1:
            x = _gmlp_layer(x, norm_g[i, 0], sc1, sh1, g1, gm_w_in[j], gm_ln_g[j], gm_ln_b[j], gm_w_s[j],
                            gm_b_s[j], gm_w_out[j])
        else:
            x = _na_layer(x, norm_g[i, 0], sc1, sh1, g1, na_w_qkv[j], na_rpb[j], na_w_out[j])
        x = _moe(x, norm_g[i, 1], sc2, sh2, g2, router_w, router_b, w_gate_up_all, w_down_all, i, final_g,
                 final=(i == depth - 1))
    return x
```
